```python
import jax, jax.numpy as jnp
from jax import lax
import numpy as np

D_MODEL = 4096
BATCH = 2
SEQ = 8192
DEPTH = 4

N_MIXERS = 2
HEAD_DIM = 128
N_Q_HEADS = D_MODEL // HEAD_DIM
N_KV_HEADS = N_Q_HEADS // 4
Q_GROUP = N_Q_HEADS // N_KV_HEADS
Q_RANK = D_MODEL // 4
KV_DIM = N_KV_HEADS * HEAD_DIM
N_IDX_HEADS = D_MODEL // 128
IDX_DIM = 64
INDEX_TOPK = 256
Q_BLOCK = 128
ROPE_THETA = 10000.0
ATT_IN = Q_RANK + 2 * KV_DIM + IDX_DIM + N_IDX_HEADS
HGRN_EXPAND = 128
HGRN_HEADS = D_MODEL // HGRN_EXPAND
FORGET_DIM = HGRN_HEADS * HGRN_EXPAND
HGRN_DV = D_MODEL // HGRN_HEADS
HGRN_CHUNK = 64
HGRN_IN = 2 * FORGET_DIM + 2 * D_MODEL
N_EXPERTS = 32
N_GROUPS = 8
EXPERTS_PER_GROUP = N_EXPERTS // N_GROUPS
TOP_K = 2
D_EXPERT = 384
MOE_BLOCK = 128
DEEPNORM_ALPHA = (2 * DEPTH) ** 0.25
DEEPNORM_BETA = (8 * DEPTH) ** -0.25
NORM_EPS = 1e-5

kernel_name = "hybrid_dsa_hgrn2_grouped_moe_deepnorm"

F32 = jnp.float32


def layer_norm(x, g, b):
    xf = x.astype(F32)
    mu = jnp.mean(xf, axis=-1, keepdims=True)
    xc = xf - mu
    var = jnp.mean(xc * xc, axis=-1, keepdims=True)
    return (xc * lax.rsqrt(var + NORM_EPS) * g.astype(F32) + b.astype(F32)).astype(x.dtype)


def rms_norm(x, g):
    xf = x.astype(F32)
    ms = jnp.mean(xf * xf, axis=-1, keepdims=True)
    return (xf * lax.rsqrt(ms + NORM_EPS) * g.astype(F32)).astype(x.dtype)


def rope(x, positions):
    d = x.shape[-1]
    half = d // 2
    inv_freq = ROPE_THETA ** (-2.0 * jnp.arange(half, dtype=F32) / d)
    ang = positions.astype(F32)[..., None] * inv_freq
    ang = ang.reshape(ang.shape[:2] + (1,) * (x.ndim - 3) + (half,))
    cos, sin = jnp.cos(ang), jnp.sin(ang)
    x1 = x[..., :half].astype(F32)
    x2 = x[..., half:].astype(F32)
    return jnp.concatenate([x1 * cos - x2 * sin, x2 * cos + x1 * sin], axis=-1).astype(x.dtype)


def dsa_sparse_attention(q, k, v, q_idx, k_idx, w_idx):
    B, T, NQ, HD = q.shape
    L = k.shape[1]
    k_sel = min(INDEX_TOPK, L // 4)
    nb = T // Q_BLOCK
    key_pos = jnp.arange(L)

    def blockify(a):
        return jnp.moveaxis(a.reshape((B, nb, Q_BLOCK) + a.shape[2:]), 1, 0)

    def one_block(args):
        qb, qib, wb, qpos = args
        logits = jnp.einsum('bqhd,bsd->bqhs', qib, k_idx)
        score = jnp.einsum('bqhs,bqh->bqs', jax.nn.relu(logits).astype(F32), wb.astype(F32))
        causal = key_pos[None, :] <= qpos[:, None]
        score = jnp.where(causal[None], score, -jnp.inf)
        _, sel = lax.top_k(score, k_sel)
        valid = sel <= qpos[None, :, None]
        kg = jax.vmap(lambda kk, ii: kk[ii])(k, sel)
        vg = jax.vmap(lambda vv, ii: vv[ii])(v, sel)
        qg = qb.reshape(B, Q_BLOCK, N_KV_HEADS, Q_GROUP, HD)
        s = jnp.einsum('bqhgd,bqkhd->bqhgk', qg, kg).astype(F32) * (HD ** -0.5)
        s = jnp.where(valid[:, :, None, None, :], s, -jnp.inf)
        p = jax.nn.softmax(s, axis=-1).astype(v.dtype)
        o = jnp.einsum('bqhgk,bqkhd->bqhgd', p, vg)
        return o.reshape(B, Q_BLOCK, NQ, HD)

    qpos_blocks = jnp.arange(T).reshape(nb, Q_BLOCK)
    out = lax.map(one_block, (blockify(q), blockify(q_idx), blockify(w_idx), qpos_blocks))
    return jnp.moveaxis(out, 0, 1).reshape(B, T, NQ * HD)


def dsa_mixer(x, positions, w_in, q_norm, w_uq, w_uq_idx, kidx_g, kidx_b, w_o):
    B, T, _ = x.shape
    cuts = [Q_RANK, Q_RANK + KV_DIM, Q_RANK + 2 * KV_DIM, Q_RANK + 2 * KV_DIM + IDX_DIM]
    c_q, k, v, k_idx, w_idx = jnp.split(x @ w_in, cuts, axis=-1)
    c_q = rms_norm(c_q, q_norm)
    q = rope((c_q @ w_uq).reshape(B, T, N_Q_HEADS, HEAD_DIM), positions)
    q_idx = rope((c_q @ w_uq_idx).reshape(B, T, N_IDX_HEADS, IDX_DIM), positions)
    k = rope(k.reshape(B, T, N_KV_HEADS, HEAD_DIM), positions)
    v = v.reshape(B, T, N_KV_HEADS, HEAD_DIM)
    k_idx = rope(layer_norm(k_idx, kidx_g, kidx_b), positions)
    w_idx = w_idx * (N_IDX_HEADS ** -0.5 * IDX_DIM ** -0.5)
    o = dsa_sparse_attention(q, k, v, q_idx, k_idx, w_idx)
    return o @ w_o


def hgrn2_chunk_recurrence(q, k, v, log_f):
    B, T, H, DK = q.shape
    DV = v.shape[-1]
    C = HGRN_CHUNK
    n = T // C

    def chunks(a):
        return a.reshape(B, n, C, H, a.shape[-1]).transpose(1, 0, 3, 2, 4)

    causal = jnp.tril(jnp.ones((C, C), dtype=bool))[None, None, :, :, None]

    def step(state, inp):
        qc, kc, vc, gc = inp
        b = jnp.cumsum(gc, axis=2)
        b_last = b[:, :, -1:, :]
        decay = jnp.exp(jnp.where(causal, b[:, :, :, None, :] - b[:, :, None, :, :], -jnp.inf))
        scores = jnp.einsum('bhtk,bhsk,bhtsk->bhts', qc, kc, decay)
        o = (jnp.einsum('bhts,bhsv->bhtv', scores, vc)
             + jnp.einsum('bhtk,bhkv->bhtv', qc * jnp.exp(b), state))
        state = (jnp.exp(b_last[:, :, 0, :])[..., None] * state
                 + jnp.einsum('bhsk,bhsv->bhkv', kc * jnp.exp(b_last - b), vc))
        return state, o

    s0 = jnp.zeros((B, H, DK, DV), F32)
    _, o = lax.scan(step, s0, (chunks(q), chunks(k), chunks(v), chunks(log_f)))
    return o.transpose(1, 0, 3, 2, 4).reshape(B, T, H, DV)


def hgrn2_mixer(x, lower_bound, w_in, out_norm, w_o):
    B, T, _ = x.shape
    cuts = [FORGET_DIM, 2 * FORGET_DIM, 2 * FORGET_DIM + D_MODEL]
    q, f, i, g = jnp.split(x @ w_in, cuts, axis=-1)
    q = jax.nn.silu(q.astype(F32)).reshape(B, T, HGRN_HEADS, HGRN_EXPAND)
    lb = lower_bound.astype(F32)
    log_f = jnp.logaddexp(jnp.log(lb), jnp.log1p(-lb) + jax.nn.log_sigmoid(f.astype(F32)))
    k = -jnp.expm1(log_f)
    shp_k = (B, T, HGRN_HEADS, HGRN_EXPAND)
    shp_v = (B, T, HGRN_HEADS, HGRN_DV)
    o = hgrn2_chunk_recurrence(q, k.reshape(shp_k), i.astype(F32).reshape(shp_v), log_f.reshape(shp_k))
    o = rms_norm(o, out_norm) * jax.nn.silu(g.astype(F32).reshape(shp_v))
    return o.reshape(B, T, D_MODEL).astype(x.dtype) @ w_o


def route_grouped(x2, router_w, router_b):
    s = jax.nn.sigmoid(x2.astype(F32) @ router_w.astype(F32))
    r = s + router_b.astype(F32)
    grp = r.reshape(-1, N_GROUPS, EXPERTS_PER_GROUP)
    grp_score = jnp.sum(lax.top_k(grp, TOP_K)[0], axis=-1)
    best = jnp.argmax(grp_score, axis=-1)
    in_group = (jnp.arange(N_EXPERTS) // EXPERTS_PER_GROUP)[None, :] == best[:, None]
    _, idx = lax.top_k(jnp.where(in_group, r, -jnp.inf), TOP_K)
    w = jnp.take_along_axis(s, idx, axis=-1)
    return idx, w / jnp.sum(w, axis=-1, keepdims=True)


def moe_ffn(h, router_w, router_b, w_gate, w_up, w_down):
    B, T, D = h.shape
    x2 = h.reshape(-1, D)
    N = x2.shape[0]
    idx, gate = route_grouped(x2, router_w, router_b)
    A = N * TOP_K
    flat_e = idx.reshape(-1)
    flat_tok = jnp.arange(A, dtype=jnp.int32) // TOP_K
    flat_g = gate.reshape(-1)
    order = jnp.argsort(flat_e)
    sorted_e = flat_e[order]
    counts = jnp.bincount(flat_e, length=N_EXPERTS)
    starts = jnp.cumsum(counts) - counts
    padded = (counts + MOE_BLOCK - 1) // MOE_BLOCK * MOE_BLOCK
    pends = jnp.cumsum(padded)
    pstarts = pends - padded
    dest = pstarts[sorted_e] + (jnp.arange(A) - starts[sorted_e])
    P = (A + N_EXPERTS * MOE_BLOCK + MOE_BLOCK - 1) // MOE_BLOCK * MOE_BLOCK
    row_tok = jnp.full((P,), N, jnp.int32).at[dest].set(flat_tok[order])
    row_gate = jnp.zeros((P,), F32).at[dest].set(flat_g[order])
    nb = P // MOE_BLOCK
    block_e = jnp.minimum(jnp.searchsorted(pends, jnp.arange(nb) * MOE_BLOCK, side='right'), N_EXPERTS - 1)
    xs = jnp.take(x2, row_tok, axis=0, mode='fill', fill_value=0).reshape(nb, MOE_BLOCK, D)

    def expert_block(args):
        xb, e = args
        hb = jax.nn.silu(xb @ w_gate[e]) * (xb @ w_up[e])
        return hb @ w_down[e]

    ys = lax.map(expert_block, (xs, block_e)).reshape(P, D)
    contrib = (ys.astype(F32) * row_gate[:, None]).astype(h.dtype)
    out = jnp.zeros_like(x2).at[row_tok].add(contrib, mode='drop')
    return out.reshape(B, T, D)


def setup_inputs(seed: int = 0) -> dict:
    key = jax.random.key(seed)
    ks = jax.random.split(key, 24)
    na = (DEPTH + 1) // 2
    nh = DEPTH // 2
    sd = D_MODEL ** -0.5

    def nrm(k, shape, scale):
        return jax.random.normal(k, shape, F32) * scale

    x = nrm(ks[0], (BATCH, SEQ, D_MODEL), 1.0)
    positions = jnp.tile(jnp.arange(SEQ, dtype=jnp.int32)[None, :], (BATCH, 1))
    v_lo = Q_RANK + KV_DIM
    att_cols = jnp.ones((ATT_IN,), F32).at[v_lo:v_lo + KV_DIM].set(DEEPNORM_BETA)
    attn_w_in = nrm(ks[1], (na, D_MODEL, ATT_IN), sd) * att_cols
    attn_q_norm = 1.0 + nrm(ks[2], (na, Q_RANK), 0.02)
    attn_w_uq = nrm(ks[3], (na, Q_RANK, N_Q_HEADS * HEAD_DIM), Q_RANK ** -0.5)
    attn_w_uq_idx = nrm(ks[4], (na, Q_RANK, N_IDX_HEADS * IDX_DIM), Q_RANK ** -0.5)
    attn_kidx_norm_g = 1.0 + nrm(ks[5], (na, IDX_DIM), 0.02)
    attn_kidx_norm_b = nrm(ks[6], (na, IDX_DIM), 0.02)
    attn_w_o = nrm(ks[7], (na, D_MODEL, D_MODEL), sd * DEEPNORM_BETA)
    i_lo = 2 * FORGET_DIM
    hg_cols = jnp.ones((HGRN_IN,), F32).at[i_lo:i_lo + D_MODEL].set(DEEPNORM_BETA)
    hgrn_w_in = nrm(ks[8], (nh, D_MODEL, HGRN_IN), sd) * hg_cols
    hgrn_lower_bounds = nrm(ks[9], (DEPTH, FORGET_DIM), 0.1)
    hgrn_out_norm = 1.0 + nrm(ks[10], (nh, HGRN_DV), 0.02)
    hgrn_w_o = nrm(ks[11], (nh, D_MODEL, D_MODEL), sd * DEEPNORM_BETA)
    ln_mix_g = 1.0 + nrm(ks[12], (DEPTH, D_MODEL), 0.02)
    ln_mix_b = nrm(ks[13], (DEPTH, D_MODEL), 0.02)
    ln_ffn_g = 1.0 + nrm(ks[14], (DEPTH, D_MODEL), 0.02)
    ln_ffn_b = nrm(ks[15], (DEPTH, D_MODEL), 0.02)
    router_w = nrm(ks[16], (D_MODEL, N_EXPERTS), sd)
    router_b = nrm(ks[17], (N_EXPERTS,), 0.01)
    moe_w_gate = nrm(ks[18], (DEPTH, N_EXPERTS, D_MODEL, D_EXPERT), sd)
    moe_w_up = nrm(ks[19], (DEPTH, N_EXPERTS, D_MODEL, D_EXPERT), sd)
    moe_w_down = nrm(ks[20], (DEPTH, N_EXPERTS, D_EXPERT, D_MODEL), D_EXPERT ** -0.5 * DEEPNORM_BETA)
    return {"x": x, "positions": positions,
            "attn_w_in": attn_w_in, "attn_q_norm": attn_q_norm, "attn_w_uq": attn_w_uq,
            "attn_w_uq_idx": attn_w_uq_idx, "attn_kidx_norm_g": attn_kidx_norm_g,
            "attn_kidx_norm_b": attn_kidx_norm_b, "attn_w_o": attn_w_o,
            "hgrn_w_in": hgrn_w_in, "hgrn_lower_bounds": hgrn_lower_bounds,
            "hgrn_out_norm": hgrn_out_norm, "hgrn_w_o": hgrn_w_o,
            "ln_mix_g": ln_mix_g, "ln_mix_b": ln_mix_b, "ln_ffn_g": ln_ffn_g, "ln_ffn_b": ln_ffn_b,
            "router_w": router_w, "router_b": router_b,
            "moe_w_gate": moe_w_gate, "moe_w_up": moe_w_up, "moe_w_down": moe_w_down}


def reference(x, positions, attn_w_in, attn_q_norm, attn_w_uq, attn_w_uq_idx, attn_kidx_norm_g,
              attn_kidx_norm_b, attn_w_o, hgrn_w_in, hgrn_lower_bounds, hgrn_out_norm, hgrn_w_o,
              ln_mix_g, ln_mix_b, ln_ffn_g, ln_ffn_b, router_w, router_b,
              moe_w_gate, moe_w_up, moe_w_down):
    lb_soft = jax.nn.softmax(hgrn_lower_bounds.astype(F32), axis=0)
    lower_bounds = jnp.cumsum(lb_soft, axis=0) - lb_soft[0]
    h = x
    for layer in range(DEPTH):
        j = layer // N_MIXERS
        if layer % N_MIXERS == 0:
            mix = dsa_mixer(h, positions, attn_w_in[j], attn_q_norm[j], attn_w_uq[j], attn_w_uq_idx[j],
                            attn_kidx_norm_g[j], attn_kidx_norm_b[j], attn_w_o[j])
        else:
            mix = hgrn2_mixer(h, lower_bounds[layer], hgrn_w_in[j], hgrn_out_norm[j], hgrn_w_o[j])
        h = layer_norm(DEEPNORM_ALPHA * h + mix, ln_mix_g[layer], ln_mix_b[layer])
        ffn = moe_ffn(h, router_w, router_b, moe_w_gate[layer], moe_w_up[layer], moe_w_down[layer])
        h = layer_norm(DEEPNORM_ALPHA * h + ffn, ln_ffn_g[layer], ln_ffn_b[layer])
    return h
```

```python
import functools
import math

import jax
import jax.numpy as jnp
from jax import lax
from jax.experimental import pallas as pl
from jax.experimental.pallas import tpu as pltpu

F32 = jnp.float32
BF16 = jnp.bfloat16

HEAD_DIM = 128
Q_GROUP = 4
IDX_DIM = 64
INDEX_TOPK = 256
ROPE_THETA = 10000.0
HGRN_EXPAND = 128
HGRN_CHUNK = 64
N_EXPERTS = 32
N_GROUPS = 8
EXPERTS_PER_GROUP = N_EXPERTS // N_GROUPS
NORM_EPS = 1e-5

LANES = 128
VMEM_LIMIT_BYTES = 56 * 1024 * 1024

NEG_BIAS = -1e30
INT_MIN = -(2 ** 31)
LOG2E = 1.4426950408889634
SAFE_DECAY_EXPONENT = 80.0

_NT = (((1,), (1,)), ((), ()))
_TN = (((0,), (0,)), ((), ()))


def _pick_tile(n, target, quantum=LANES):
    if n <= target:
        return n
    t = (target // quantum) * quantum
    while t > quantum and n % t:
        t -= quantum
    assert n % t == 0, (n, target)
    return t


def _params(*sem):
    return pltpu.CompilerParams(dimension_semantics=sem, vmem_limit_bytes=VMEM_LIMIT_BYTES)


def _sigmoid(x):
    return 1.0 / (1.0 + jnp.exp(-x))


def _mm_kernel(a_ref, b_ref, o_ref, abf_ref):
    @pl.when(pl.program_id(1) == 0)
    def _():
        abf_ref[...] = a_ref[...].astype(BF16)

    o_ref[...] = jnp.dot(abf_ref[...], b_ref[...], preferred_element_type=F32).astype(o_ref.dtype)


def matmul(a, b, out_dtype=F32, tm=512, tn=1024):
    m, k = a.shape
    _, n = b.shape
    tm = _pick_tile(m, tm, 8)
    tn = _pick_tile(n, tn)
    return pl.pallas_call(
        _mm_kernel,
        grid=(m // tm, n // tn),
        in_specs=[pl.BlockSpec((tm, k), lambda i, j: (i, 0)),
                  pl.BlockSpec((k, tn), lambda i, j: (0, j))],
        out_specs=pl.BlockSpec((tm, tn), lambda i, j: (i, j)),
        out_shape=jax.ShapeDtypeStruct((m, n), out_dtype),
        scratch_shapes=[pltpu.VMEM((tm, k), BF16)],
        compiler_params=_params("parallel", "arbitrary"),
        name="matmul",
    )(a, b)


def _layer_norm_rows(z, g, b):
    mu = jnp.mean(z, axis=-1, keepdims=True)
    zc = z - mu
    var = jnp.mean(zc * zc, axis=-1, keepdims=True)
    return zc * lax.rsqrt(var + NORM_EPS) * g + b


def _res_ln_kernel(h_ref, y_ref, g_ref, b_ref, o_ref, *, alpha):
    o_ref[...] = _layer_norm_rows(alpha * h_ref[...] + y_ref[...], g_ref[...], b_ref[...])


def residual_layer_norm(h, y, g, b, alpha, tm=256):
    n, d = h.shape
    tm = _pick_tile(n, tm, 8)
    row = pl.BlockSpec((tm, d), lambda i: (i, 0))
    vec = pl.BlockSpec((1, d), lambda i: (0, 0))
    return pl.pallas_call(
        functools.partial(_res_ln_kernel, alpha=alpha),
        grid=(n // tm,),
        in_specs=[row, row, vec, vec],
        out_specs=row,
        out_shape=jax.ShapeDtypeStruct((n, d), F32),
        compiler_params=_params("parallel"),
        name="residual_layer_norm",
    )(h, y, g.reshape(1, d), b.reshape(1, d))


def _rope_tab_kernel(pos_ref, f128_ref, f64_ref, c128_ref, s128_ref, c64_ref, s64_ref):
    pos = pos_ref[...]
    lane = lax.broadcasted_iota(jnp.int32, (1, LANES), 1)
    for hd, f_ref, c_ref, s_ref in ((HEAD_DIM, f128_ref, c128_ref, s128_ref),
                                    (IDX_DIM, f64_ref, c64_ref, s64_ref)):
        ang = pos * f_ref[...]
        sign = jnp.where((lane % hd) < hd // 2, -1.0, 1.0)
        c_ref[...] = jnp.cos(ang)
        s_ref[...] = jnp.sin(ang) * sign


def rope_tables(positions):
    n = positions.size
    tm = _pick_tile(n, 512, 8)

    def freqs(hd):
        half = hd // 2
        inv = ROPE_THETA ** (-2.0 * jnp.arange(half, dtype=F32) / hd)
        return jnp.tile(inv, LANES // half).reshape(1, LANES)

    tab = pl.BlockSpec((tm, LANES), lambda i: (i, 0))
    vec = pl.BlockSpec((1, LANES), lambda i: (0, 0))
    shape = jax.ShapeDtypeStruct((n, LANES), F32)
    return pl.pallas_call(
        _rope_tab_kernel,
        grid=(n // tm,),
        in_specs=[pl.BlockSpec((tm, 1), lambda i: (i, 0)), vec, vec],
        out_specs=[tab, tab, tab, tab],
        out_shape=[shape, shape, shape, shape],
        compiler_params=_params("parallel"),
        name="rope_tables",
    )(positions.reshape(n, 1).astype(F32), freqs(HEAD_DIM), freqs(IDX_DIM))


def _rotate_half(x, hd):
    if hd == LANES:
        return pltpu.roll(x, LANES // 2, 1)
    lane = lax.broadcasted_iota(jnp.int32, x.shape, 1)
    half = hd // 2
    return jnp.where((lane % hd) < half, pltpu.roll(x, LANES - half, 1), pltpu.roll(x, half, 1))


def _qprep_kernel(cq_ref, g_ref, w_ref, cos_ref, sin_ref, o_ref, xn_ref, *, hd, scale):
    @pl.when(pl.program_id(1) == 0)
    def _():
        x = cq_ref[...]
        ms = jnp.mean(x * x, axis=-1, keepdims=True)
        xn_ref[...] = (x * lax.rsqrt(ms + NORM_EPS) * g_ref[...]).astype(BF16)

    y = jnp.dot(xn_ref[...], w_ref[...], preferred_element_type=F32)
    cos = cos_ref[...]
    sin = sin_ref[...]
    for c in range(y.shape[1] // LANES):
        x = y[:, c * LANES:(c + 1) * LANES]
        r = x * cos + _rotate_half(x, hd) * sin
        o_ref[:, c * LANES:(c + 1) * LANES] = (r * scale).astype(o_ref.dtype)


def query_projection(proj, q_rank, q_norm, w, cos, sin, hd, scale, tm=512, tn=1024):
    n = proj.shape[0]
    nout = w.shape[1]
    tm = _pick_tile(n, tm, 8)
    tn = _pick_tile(nout, tn)
    tab = pl.BlockSpec((tm, LANES), lambda i, j: (i, 0))
    return pl.pallas_call(
        functools.partial(_qprep_kernel, hd=hd, scale=scale),
        grid=(n // tm, nout // tn),
        in_specs=[pl.BlockSpec((tm, q_rank), lambda i, j: (i, 0)),
                  pl.BlockSpec((1, q_rank), lambda i, j: (0, 0)),
                  pl.BlockSpec((q_rank, tn), lambda i, j: (0, j)),
                  tab, tab],
        out_specs=pl.BlockSpec((tm, tn), lambda i, j: (i, j)),
        out_shape=jax.ShapeDtypeStruct((n, nout), BF16),
        scratch_shapes=[pltpu.VMEM((tm, q_rank), BF16)],
        compiler_params=_params("parallel", "arbitrary"),
        name="query_projection",
    )(proj, q_norm.reshape(1, q_rank), w, cos, sin)


def _kprep_kernel(k_ref, v_ref, t_ref, g_ref, b_ref, c128_ref, s128_ref, c64_ref, s64_ref,
                  ko_ref, vo_ref, ka_ref, kb_ref, w_ref, *, w_scale):
    cos = c128_ref[...]
    sin = s128_ref[...]
    for c in range(k_ref.shape[1] // LANES):
        sl = slice(c * LANES, (c + 1) * LANES)
        x = k_ref[:, sl]
        ko_ref[:, sl] = (x * cos + _rotate_half(x, HEAD_DIM) * sin).astype(BF16)
    vo_ref[...] = v_ref[...].astype(BF16)

    t = t_ref[...]
    lane = lax.broadcasted_iota(jnp.int32, t.shape, 1)
    is_key = lane < IDX_DIM
    mu = jnp.sum(jnp.where(is_key, t, 0.0), axis=-1, keepdims=True) * (1.0 / IDX_DIM)
    tc = jnp.where(is_key, t - mu, 0.0)
    var = jnp.sum(tc * tc, axis=-1, keepdims=True) * (1.0 / IDX_DIM)
    y = tc * lax.rsqrt(var + NORM_EPS) * g_ref[...] + b_ref[...]
    y = y * c64_ref[...] + _rotate_half(y, IDX_DIM) * s64_ref[...]
    ka_ref[...] = y.astype(BF16)
    kb_ref[...] = pltpu.roll(y, IDX_DIM, 1).astype(BF16)
    w_ref[...] = pltpu.roll(t, IDX_DIM, 1) * w_scale


def key_value_projection(proj, q_rank, kv_dim, kidx_g, kidx_b, tabs, w_scale, tm=512):
    n = proj.shape[0]
    tm = _pick_tile(n, tm, 8)
    assert q_rank % kv_dim == 0 and (q_rank + 2 * kv_dim) % LANES == 0
    kb0 = q_rank // kv_dim
    tail_blk = (q_rank + 2 * kv_dim) // LANES
    pad = LANES - IDX_DIM
    g = jnp.pad(kidx_g, (0, pad)).reshape(1, LANES)
    b = jnp.pad(kidx_b, (0, pad)).reshape(1, LANES)
    tab = pl.BlockSpec((tm, LANES), lambda i: (i, 0))
    vec = pl.BlockSpec((1, LANES), lambda i: (0, 0))
    wide = pl.BlockSpec((tm, kv_dim), lambda i: (i, 0))
    return pl.pallas_call(
        functools.partial(_kprep_kernel, w_scale=w_scale),
        grid=(n // tm,),
        in_specs=[pl.BlockSpec((tm, kv_dim), lambda i: (i, kb0)),
                  pl.BlockSpec((tm, kv_dim), lambda i: (i, kb0 + 1)),
                  pl.BlockSpec((tm, LANES), lambda i: (i, tail_blk)),
                  vec, vec, tab, tab, tab, tab],
        out_specs=[wide, wide, tab, tab, tab],
        out_shape=[jax.ShapeDtypeStruct((n, kv_dim), BF16), jax.ShapeDtypeStruct((n, kv_dim), BF16),
                   jax.ShapeDtypeStruct((n, LANES), BF16), jax.ShapeDtypeStruct((n, LANES), BF16),
                   jax.ShapeDtypeStruct((n, LANES), F32)],
        compiler_params=_params("parallel"),
        name="key_value_projection",
    )(proj, proj, proj, g, b, *tabs)


def _indexer_kernel(q_ref, w_ref, ka_ref, kb_ref, o_ref, key_ref, cut_ref, *, n_heads, k_sel, seq):
    nk, tq, tk = key_ref.shape
    qi = pl.program_id(1)
    n_live = qi + 1
    row = lax.broadcasted_iota(jnp.int32, (tq, tk), 0) + qi * tq
    col0 = lax.broadcasted_iota(jnp.int32, (tq, tk), 1)
    w = w_ref[...]

    def score_block(kj, carry):
        off = pl.multiple_of(kj * tk, tk)
        ka = ka_ref[pl.ds(off, tk), :]
        kb = kb_ref[pl.ds(off, tk), :]
        acc = jnp.zeros((tq, tk), F32)
        for p in range(n_heads // 2):
            qp = q_ref[:, p * LANES:(p + 1) * LANES]
            la = lax.dot_general(qp, ka, _NT, preferred_element_type=F32)
            lb = lax.dot_general(qp, kb, _NT, preferred_element_type=F32)
            acc = acc + jnp.maximum(la, 0.0) * w[:, 2 * p:2 * p + 1]
            acc = acc + jnp.maximum(lb, 0.0) * w[:, 2 * p + 1:2 * p + 2]
        s = jnp.where(col0 + kj * tk <= row, acc, -jnp.inf)
        bits = pltpu.bitcast(s, jnp.int32)
        key_ref[kj] = bits ^ ((bits >> 31) & 0x7FFFFFFF)
        return carry

    lax.fori_loop(0, n_live, score_block, 0)

    lanes_per_blk = tk // LANES
    lane = lax.broadcasted_iota(jnp.int32, (tq, LANES), 1)

    def count(pred):
        def body(kj, c):
            for l in range(lanes_per_blk):
                key = key_ref[kj, :, l * LANES:(l + 1) * LANES]
                col = lane + (kj * tk + l * LANES)
                c = c + jnp.where(pred(key, col), 1.0, 0.0)
            return c
        c = lax.fori_loop(0, n_live, body, jnp.zeros((tq, LANES), F32))
        return jnp.broadcast_to(jnp.sum(c, axis=1, keepdims=True), (tq, LANES))

    def bit_step(i, u):
        cand = u | lax.shift_left(jnp.int32(1), 31 - i)
        cand_signed = cand ^ INT_MIN
        cnt = count(lambda key, col: key >= cand_signed)
        return jnp.where(cnt >= k_sel, cand, u)

    u = lax.fori_loop(0, 32, bit_step, jnp.zeros((tq, LANES), jnp.int32))
    thr = u ^ INT_MIN

    n_gt = count(lambda key, col: key > thr)
    n_ge = count(lambda key, col: key >= thr)

    cut_ref[...] = jnp.full((tq, LANES), seq, jnp.int32)

    @pl.when(jnp.max(n_ge) > k_sel)
    def _():
        need = k_sel - n_gt
        n_bits = max(1, (seq - 1).bit_length())

        def idx_step(i, p):
            t = p + lax.shift_left(jnp.int32(1), n_bits - 1 - i)
            cnt = count(lambda key, col: (key == thr) & (col < t))
            return jnp.where(cnt < need, t, p)

        cut_ref[...] = lax.fori_loop(0, n_bits, idx_step, jnp.zeros((tq, LANES), jnp.int32))

    cut = cut_ref[...]

    def emit(kj, carry):
        key = key_ref[kj]
        col = col0 + kj * tk
        thr_b = jnp.concatenate([thr] * lanes_per_blk, axis=1)
        cut_b = jnp.concatenate([cut] * lanes_per_blk, axis=1)
        keep = (key > thr_b) | ((key == thr_b) & (col <= cut_b))
        keep = keep & (col <= row)
        o_ref[kj] = jnp.where(keep, 0.0, NEG_BIAS).astype(BF16)
        return carry

    lax.fori_loop(0, n_live, emit, 0)

    def emit_dead(kj, carry):
        o_ref[kj] = jnp.full((tq, tk), NEG_BIAS, BF16)
        return carry

    lax.fori_loop(n_live, nk, emit_dead, 0)


def indexer_bias(q_idx, w_idx, k_a, k_b, batch, seq, n_heads, k_sel, tq=256):
    tq = min(tq, seq)
    nq = seq // tq
    nk = nq
    kspec = pl.BlockSpec((seq, LANES), lambda b, i: (b, 0))
    return pl.pallas_call(
        functools.partial(_indexer_kernel, n_heads=n_heads, k_sel=k_sel, seq=seq),
        grid=(batch, nq),
        in_specs=[pl.BlockSpec((tq, n_heads * IDX_DIM), lambda b, i: (b * nq + i, 0)),
                  pl.BlockSpec((tq, LANES), lambda b, i: (b * nq + i, 0)),
                  kspec, kspec],
        out_specs=pl.BlockSpec((None, nk, tq, tq), lambda b, i: (b * nq + i, 0, 0, 0)),
        out_shape=jax.ShapeDtypeStruct((batch * nq, nk, tq, tq), BF16),
        scratch_shapes=[pltpu.VMEM((nk, tq, tq), jnp.int32), pltpu.VMEM((tq, LANES), jnp.int32)],
        compiler_params=_params("parallel", "arbitrary"),
        name="indexer_bias",
    )(q_idx, w_idx, k_a, k_b)


def _attn_kernel(q_ref, k_ref, v_ref, b_ref, o_ref, m_ref, l_ref, acc_ref):
    _, tq, tk = b_ref.shape
    qi = pl.program_id(2)
    m_ref[...] = jnp.full(m_ref.shape, NEG_BIAS, F32)
    l_ref[...] = jnp.zeros(l_ref.shape, F32)
    acc_ref[...] = jnp.zeros(acc_ref.shape, F32)

    def body(kj, carry):
        off = pl.multiple_of(kj * tk, tk)
        k = k_ref[pl.ds(off, tk), :]
        v = v_ref[pl.ds(off, tk), :]
        bias = b_ref[kj].astype(F32)
        for h in range(Q_GROUP):
            q = q_ref[:, h * HEAD_DIM:(h + 1) * HEAD_DIM]
            s = lax.dot_general(q, k, _NT, preferred_element_type=F32) + bias
            m_prev = m_ref[h]
            m_new = jnp.maximum(m_prev, jnp.max(s, axis=1, keepdims=True))
            alpha = jnp.exp2(m_prev - m_new)
            p = jnp.exp2(s - m_new[:, :1])
            l_ref[h] = alpha * l_ref[h] + jnp.sum(p, axis=1, keepdims=True)
            acc_ref[h] = alpha * acc_ref[h] + jnp.dot(p.astype(BF16), v, preferred_element_type=F32)
            m_ref[h] = m_new
        return carry

    lax.fori_loop(0, qi + 1, body, 0)
    for h in range(Q_GROUP):
        o_ref[:, h * HEAD_DIM:(h + 1) * HEAD_DIM] = (acc_ref[h] / l_ref[h]).astype(o_ref.dtype)


def masked_attention(q, k, v, bias, batch, seq, n_kv):
    _, nk, tq, tk = bias.shape
    nq = seq // tq
    gw = Q_GROUP * HEAD_DIM
    kv_spec = pl.BlockSpec((seq, HEAD_DIM), lambda b, g, i: (b, g))
    q_spec = pl.BlockSpec((tq, gw), lambda b, g, i: (b * nq + i, g))
    return pl.pallas_call(
        _attn_kernel,
        grid=(batch, n_kv, nq),
        in_specs=[q_spec, kv_spec, kv_spec,
                  pl.BlockSpec((None, nk, tq, tk), lambda b, g, i: (b * nq + i, 0, 0, 0))],
        out_specs=q_spec,
        out_shape=jax.ShapeDtypeStruct(q.shape, BF16),
        scratch_shapes=[pltpu.VMEM((Q_GROUP, tq, LANES), F32), pltpu.VMEM((Q_GROUP, tq, LANES), F32),
                        pltpu.VMEM((Q_GROUP, tq, HEAD_DIM), F32)],
        compiler_params=_params("parallel", "parallel", "arbitrary"),
        name="masked_attention",
    )(q, k, v, bias)


def _hgrn_kernel(q_ref, f_ref, i_ref, g_ref, lb_ref, on_ref, o_ref, st_ref, intra_ref, *, chunk):
    tb = q_ref.shape[0]
    n_heads = q_ref.shape[1] // HGRN_EXPAND
    c_rows = chunk

    @pl.when(pl.program_id(2) == 0)
    def _():
        st_ref[...] = jnp.zeros(st_ref.shape, F32)

    rr = lax.broadcasted_iota(jnp.int32, (c_rows, c_rows), 0)
    cc = lax.broadcasted_iota(jnp.int32, (c_rows, c_rows), 1)
    tril = cc <= rr
    rowv = lax.broadcasted_iota(jnp.int32, (c_rows, HGRN_EXPAND), 0)
    onorm = on_ref[...]

    def do_chunk(c, carry):
        r0 = pl.multiple_of(c * c_rows, c_rows)
        for h in range(n_heads):
            sl = slice(h * HGRN_EXPAND, (h + 1) * HGRN_EXPAND)
            qr = q_ref[pl.ds(r0, c_rows), sl]
            fr = f_ref[pl.ds(r0, c_rows), sl]
            v = i_ref[pl.ds(r0, c_rows), sl]
            gr = g_ref[pl.ds(r0, c_rows), sl]
            lb = lb_ref[:, sl]

            q = qr * _sigmoid(qr)
            e = jnp.exp(-jnp.abs(fr))
            r = 1.0 / (1.0 + e)
            sig_pos = jnp.where(fr >= 0, r, e * r)
            sig_neg = jnp.where(fr >= 0, e * r, r)
            log_f = jnp.log(lb + (1.0 - lb) * sig_pos)
            kk = (1.0 - lb) * sig_neg

            b = log_f
            s = 1
            while s < c_rows:
                b = b + jnp.where(rowv >= s, pltpu.roll(b, s, 0), 0.0)
                s *= 2
            b_last = b[c_rows - 1:c_rows, :]

            qt = q * jnp.exp(b)
            st = st_ref[h]
            v_bf = v.astype(BF16)
            qt_bf = qt.astype(BF16)

            safe = jnp.min(b_last) >= -SAFE_DECAY_EXPONENT

            @pl.when(safe)
            def _():
                kh = (kk * jnp.exp(-b)).astype(BF16)
                a = lax.dot_general(qt_bf, kh, _NT, preferred_element_type=F32)
                a = jnp.where(tril, a, 0.0)
                intra_ref[...] = jnp.dot(a.astype(BF16), v_bf, preferred_element_type=F32)

            @pl.when(jnp.logical_not(safe))
            def _():
                def off_step(d, o):
                    kd = pltpu.roll(kk, d, 0)
                    bd = pltpu.roll(b, d, 0)
                    vd = pltpu.roll(v, d, 0)
                    wgt = q * kd * jnp.exp(jnp.minimum(b - bd, 0.0))
                    a = jnp.sum(wgt, axis=1, keepdims=True)
                    return o + jnp.where(rowv >= d, a, 0.0) * vd
                intra_ref[...] = lax.fori_loop(0, c_rows, off_step, jnp.zeros((c_rows, HGRN_EXPAND), F32))

            o = intra_ref[...] + lax.dot_general(qt_bf, st.astype(BF16), _NT, preferred_element_type=F32)
            k_new = (kk * jnp.exp(b_last - b)).astype(BF16)
            st_ref[h] = st * jnp.exp(b_last) + lax.dot_general(v_bf, k_new, _TN, preferred_element_type=F32)

            ms = jnp.mean(o * o, axis=-1, keepdims=True)
            out = o * lax.rsqrt(ms + NORM_EPS) * onorm * (gr * _sigmoid(gr))
            o_ref[pl.ds(r0, c_rows), sl] = out.astype(o_ref.dtype)
        return carry

    lax.fori_loop(0, tb // c_rows, do_chunk, 0)


def hgrn_recurrence(proj, lower_bound, out_norm, batch, seq, d_model, tb=512, heads_per_step=4):
    n = proj.shape[0]
    tb = min(tb, seq)
    nt = seq // tb
    width = min(heads_per_step * HGRN_EXPAND, d_model)
    ng = d_model // width

    def part(p):
        return pl.BlockSpec((tb, width), lambda b, g, t: (b * nt + t, p * ng + g))

    return pl.pallas_call(
        functools.partial(_hgrn_kernel, chunk=min(HGRN_CHUNK, tb)),
        grid=(batch, ng, nt),
        in_specs=[part(0), part(1), part(2), part(3),
                  pl.BlockSpec((1, width), lambda b, g, t: (0, g)),
                  pl.BlockSpec((1, HGRN_EXPAND), lambda b, g, t: (0, 0))],
        out_specs=pl.BlockSpec((tb, width), lambda b, g, t: (b * nt + t, g)),
        out_shape=jax.ShapeDtypeStruct((n, d_model), BF16),
        scratch_shapes=[pltpu.VMEM((width // HGRN_EXPAND, HGRN_EXPAND, HGRN_EXPAND), F32),
                        pltpu.VMEM((min(HGRN_CHUNK, tb), HGRN_EXPAND), F32)],
        compiler_params=_params("parallel", "parallel", "arbitrary"),
        name="hgrn_recurrence",
    )(proj, proj, proj, proj, lower_bound.reshape(1, d_model), out_norm.reshape(1, HGRN_EXPAND))


def _router_kernel(h_ref, rw_ref, rb_ref, o_ref):
    logits = lax.dot_general(rw_ref[...], h_ref[...].astype(BF16), _NT, preferred_element_type=F32)
    s_all = _sigmoid(logits)
    r_all = s_all + rb_ref[...]
    tm = logits.shape[1]

    def rows(x, g):
        return [x[g * EXPERTS_PER_GROUP + j:g * EXPERTS_PER_GROUP + j + 1, :] for j in range(EXPERTS_PER_GROUP)]

    best_score = None
    for g in range(N_GROUPS):
        a, b, c, d = rows(r_all, g)
        hi_ab, lo_ab = jnp.maximum(a, b), jnp.minimum(a, b)
        hi_cd, lo_cd = jnp.maximum(c, d), jnp.minimum(c, d)
        top1 = jnp.maximum(hi_ab, hi_cd)
        top2 = jnp.maximum(jnp.minimum(hi_ab, hi_cd), jnp.maximum(lo_ab, lo_cd))
        score = top1 + top2
        if g == 0:
            best_score, best = score, jnp.zeros((1, tm), F32)
            best_r, best_s = rows(r_all, 0), rows(s_all, 0)
        else:
            upd = score > best_score
            best_score = jnp.where(upd, score, best_score)
            best = jnp.where(upd, float(g), best)
            best_r = [jnp.where(upd, n, o) for n, o in zip(rows(r_all, g), best_r)]
            best_s = [jnp.where(upd, n, o) for n, o in zip(rows(s_all, g), best_s)]

    picked = []
    for j in range(EXPERTS_PER_GROUP):
        rank = jnp.zeros((1, tm), F32)
        for i in range(EXPERTS_PER_GROUP):
            if i == j:
                continue
            ahead = (best_r[i] > best_r[j]) | ((best_r[i] == best_r[j]) & (i < j))
            rank = rank + jnp.where(ahead, 1.0, 0.0)
        picked.append(jnp.where(rank < 2.0, best_s[j], 0.0))
    total = picked[0] + picked[1] + picked[2] + picked[3]
    out_rows = [p / total for p in picked] + [best] + [jnp.zeros((1, tm), F32)] * 3
    o_ref[...] = jnp.concatenate(out_rows, axis=0)


def route(h, router_w_t, router_b, tm=512):
    n, d = h.shape
    tm = _pick_tile(n, tm)
    return pl.pallas_call(
        _router_kernel,
        grid=(n // tm,),
        in_specs=[pl.BlockSpec((tm, d), lambda i: (i, 0)),
                  pl.BlockSpec((N_EXPERTS, d), lambda i: (0, 0)),
                  pl.BlockSpec((N_EXPERTS, 1), lambda i: (0, 0))],
        out_specs=pl.BlockSpec((8, tm), lambda i: (0, i)),
        out_shape=jax.ShapeDtypeStruct((8, n), F32),
        compiler_params=_params("parallel"),
        name="route",
    )(h, router_w_t, router_b.reshape(N_EXPERTS, 1))


def _moe_kernel(grp_ref, cnt_ref, tok_ref, x_hbm, gate_ref, wg_ref, wu_ref, wd_ref, lng_ref, lnb_ref,
                out_hbm, xg_ref, xb_ref, acc_ref, sem_in, sem_out, *, n_tokens, alpha):
    del grp_ref
    tm = xg_ref.shape[0]
    blk = pl.program_id(0)
    e = pl.program_id(1)
    n_valid = cnt_ref[blk]
    active = n_valid > 0
    base = blk * tm

    @pl.when(active & (e == 0))
    def _():
        def issue(r, carry):
            src = jnp.minimum(tok_ref[base + r], n_tokens - 1)
            pltpu.make_async_copy(x_hbm.at[pl.ds(src, 1)], xg_ref.at[pl.ds(r, 1)], sem_in).start()
            return carry
        lax.fori_loop(0, tm, issue, 0)
        pltpu.make_async_copy(x_hbm.at[pl.ds(0, tm)], xg_ref, sem_in).wait()
        xb_ref[...] = xg_ref[...].astype(BF16)
        acc_ref[...] = jnp.zeros(acc_ref.shape, F32)

    @pl.when(active)
    def _():
        x = xb_ref[...]
        hg = jnp.dot(x, wg_ref[...], preferred_element_type=F32)
        hu = jnp.dot(x, wu_ref[...], preferred_element_type=F32)
        hm = (hg * _sigmoid(hg) * hu).astype(BF16)
        y = jnp.dot(hm, wd_ref[...], preferred_element_type=F32)
        acc_ref[...] += y * gate_ref[...]

    @pl.when(active & (e == EXPERTS_PER_GROUP - 1))
    def _():
        acc_ref[...] = _layer_norm_rows(alpha * xg_ref[...] + acc_ref[...], lng_ref[...], lnb_ref[...])

        def put(r, carry):
            tok = tok_ref[base + r]
            pltpu.make_async_copy(acc_ref.at[pl.ds(r, 1)], out_hbm.at[pl.ds(tok, 1)], sem_out).start()
            return carry
        lax.fori_loop(0, n_valid, put, 0)

        def drain(r, carry):
            pltpu.make_async_copy(acc_ref.at[pl.ds(0, 1)], out_hbm.at[pl.ds(0, 1)], sem_out).wait()
            return carry
        lax.fori_loop(0, n_valid, drain, 0)


def moe_ffn_layer_norm(h, routing, wg, wu, wd, layer, ln_g, ln_b, alpha, tm=512):
    n, d = h.shape
    tm = min(tm, n)
    gates = routing[:EXPERTS_PER_GROUP]
    grp = routing[EXPERTS_PER_GROUP].astype(jnp.int32)

    onehot = (grp[:, None] == jnp.arange(N_GROUPS, dtype=jnp.int32)[None, :]).astype(jnp.int32)
    csum = jnp.cumsum(onehot, axis=0)
    counts = csum[-1]
    rank = jnp.take_along_axis(csum, grp[:, None], axis=1)[:, 0] - 1
    padded = (counts + tm - 1) // tm * tm
    pends = jnp.cumsum(padded)
    dest = (pends - padded)[grp] + rank
    p_rows = n + N_GROUPS * tm
    nb = p_rows // tm
    row_tok = jnp.full((p_rows,), n, jnp.int32).at[dest].set(jnp.arange(n, dtype=jnp.int32))
    blk_start = jnp.arange(nb, dtype=jnp.int32) * tm
    blk_grp = jnp.minimum(jnp.searchsorted(pends, blk_start, side='right'), N_GROUPS - 1).astype(jnp.int32)
    grp_end = (pends - padded + counts)[blk_grp]
    blk_cnt = jnp.where(blk_start < pends[-1], jnp.clip(grp_end - blk_start, 0, tm), 0).astype(jnp.int32)
    gate_sorted = jnp.take(gates, jnp.minimum(row_tok, n - 1), axis=1)
    gate_sorted = jnp.where((row_tok < n)[None, :], gate_sorted, 0.0).reshape(EXPERTS_PER_GROUP, p_rows, 1)

    d_exp = wg.shape[-1]
    last = EXPERTS_PER_GROUP - 1

    def w_idx(blk, e, grp_ref, cnt_ref, tok_ref):
        act = jnp.minimum(cnt_ref[blk], 1)
        return layer * N_EXPERTS + grp_ref[blk] * EXPERTS_PER_GROUP + e * act + last * (1 - act)

    grid_spec = pltpu.PrefetchScalarGridSpec(
        num_scalar_prefetch=3,
        grid=(nb, EXPERTS_PER_GROUP),
        in_specs=[pl.BlockSpec(memory_space=pl.ANY),
                  pl.BlockSpec((None, tm, 1), lambda blk, e, *_: (e, blk, 0)),
                  pl.BlockSpec((None, d, d_exp), lambda blk, e, *s: (w_idx(blk, e, *s), 0, 0)),
                  pl.BlockSpec((None, d, d_exp), lambda blk, e, *s: (w_idx(blk, e, *s), 0, 0)),
                  pl.BlockSpec((None, d_exp, d), lambda blk, e, *s: (w_idx(blk, e, *s), 0, 0)),
                  pl.BlockSpec((1, d), lambda blk, e, *_: (0, 0)),
                  pl.BlockSpec((1, d), lambda blk, e, *_: (0, 0))],
        out_specs=pl.BlockSpec(memory_space=pl.ANY),
        scratch_shapes=[pltpu.VMEM((tm, d), F32), pltpu.VMEM((tm, d), BF16), pltpu.VMEM((tm, d), F32),
                        pltpu.SemaphoreType.DMA(()), pltpu.SemaphoreType.DMA(())],
    )
    return pl.pallas_call(
        functools.partial(_moe_kernel, n_tokens=n, alpha=alpha),
        grid_spec=grid_spec,
        out_shape=jax.ShapeDtypeStruct((n, d), F32),
        compiler_params=_params("arbitrary", "arbitrary"),
        name="moe_ffn_layer_norm",
    )(blk_grp, blk_cnt, row_tok, h, gate_sorted, wg, wu, wd, ln_g.reshape(1, d), ln_b.reshape(1, d))


def kernel(x, positions, attn_w_in, attn_q_norm, attn_w_uq, attn_w_uq_idx, attn_kidx_norm_g, attn_kidx_norm_b, attn_w_o, hgrn_w_in, hgrn_lower_bounds, hgrn_out_norm, hgrn_w_o, ln_mix_g, ln_mix_b, ln_ffn_g, ln_ffn_b, router_w, router_b, moe_w_gate, moe_w_up, moe_w_down):
    batch, seq, d = x.shape
    n = batch * seq
    depth = ln_mix_g.shape[0]
    alpha = (2 * depth) ** 0.25
    q_rank = attn_w_uq.shape[1]
    n_q = attn_w_uq.shape[2] // HEAD_DIM
    n_kv = n_q // Q_GROUP
    kv_dim = n_kv * HEAD_DIM
    n_idx = attn_w_uq_idx.shape[2] // IDX_DIM
    k_sel = min(INDEX_TOPK, seq // 4)
    att_in = attn_w_in.shape[2]
    att_pad = q_rank + 2 * kv_dim + LANES - att_in

    w_in_a = jnp.pad(attn_w_in, ((0, 0), (0, 0), (0, att_pad))).astype(BF16)
    w_uq = attn_w_uq.astype(BF16)
    w_uq_idx = attn_w_uq_idx.astype(BF16)
    w_o_a = attn_w_o.astype(BF16)
    w_in_h = hgrn_w_in.astype(BF16)
    w_o_h = hgrn_w_o.astype(BF16)
    d_exp = moe_w_gate.shape[-1]
    wg = moe_w_gate.astype(BF16).reshape(depth * N_EXPERTS, d, d_exp)
    wu = moe_w_up.astype(BF16).reshape(depth * N_EXPERTS, d, d_exp)
    wd = moe_w_down.astype(BF16).reshape(depth * N_EXPERTS, d_exp, d)
    router_w_t = router_w.T.astype(BF16)
    lb_soft = jax.nn.softmax(hgrn_lower_bounds.astype(F32), axis=0)
    lower_bounds = jnp.cumsum(lb_soft, axis=0) - lb_soft[0]

    c128, s128, c64, s64 = rope_tables(positions)

    h = x.reshape(n, d)
    for layer in range(depth):
        j = layer // 2
        if layer % 2 == 0:
            proj = matmul(h, w_in_a[j], tn=640)
            q = query_projection(proj, q_rank, attn_q_norm[j], w_uq[j], c128, s128, HEAD_DIM,
                                 HEAD_DIM ** -0.5 * LOG2E)
            q_idx = query_projection(proj, q_rank, attn_q_norm[j], w_uq_idx[j], c64, s64, IDX_DIM, 1.0)
            k, v, k_a, k_b, w_idx = key_value_projection(
                proj, q_rank, kv_dim, attn_kidx_norm_g[j], attn_kidx_norm_b[j], (c128, s128, c64, s64),
                n_idx ** -0.5 * IDX_DIM ** -0.5)
            bias = indexer_bias(q_idx, w_idx, k_a, k_b, batch, seq, n_idx, k_sel)
            o = masked_attention(q, k, v, bias, batch, seq, n_kv)
            mix = matmul(o, w_o_a[j])
        else:
            proj = matmul(h, w_in_h[j])
            o = hgrn_recurrence(proj, lower_bounds[layer], hgrn_out_norm[j], batch, seq, d)
            mix = matmul(o, w_o_h[j])
        h = residual_layer_norm(h, mix, ln_mix_g[layer], ln_mix_b[layer], alpha)
        routing = route(h, router_w_t, router_b)
        h = moe_ffn_layer_norm(h, routing, wg, wu, wd, layer, ln_ffn_g[layer], ln_ffn_b[layer], alpha)
    return h.reshape(batch, seq, d)
```

```python
import functools
import math

import jax
import jax.numpy as jnp
from jax import lax
from jax.experimental import pallas as pl
from jax.experimental.pallas import tpu as pltpu

F32 = jnp.float32
BF16 = jnp.bfloat16

HEAD_DIM = 128
Q_GROUP = 4
IDX_DIM = 64
INDEX_TOPK = 256
ROPE_THETA = 10000.0
HGRN_EXPAND = 128
HGRN_CHUNK = 64
ATTN_BLOCK = 256
N_EXPERTS = 32
N_GROUPS = 8
EXPERTS_PER_GROUP = N_EXPERTS // N_GROUPS
NORM_EPS = 1e-5

LANES = 128
SUBLANES = 8
VMEM_LIMIT_BYTES = 56 * 1024 * 1024

NEG_BIAS = -1e30
INT_MIN = -(2 ** 31)
LOG2E = 1.4426950408889634
SAFE_DECAY_EXPONENT = 80.0

_NT = (((1,), (1,)), ((), ()))
_TN = (((0,), (0,)), ((), ()))


def _pick_tile(n, target, quantum=LANES):
    if n <= target:
        return n
    t = (target // quantum) * quantum
    while t > quantum and n % t:
        t -= quantum
    assert n % t == 0, (n, target)
    return t


def _params(*sem):
    return pltpu.CompilerParams(dimension_semantics=sem, vmem_limit_bytes=VMEM_LIMIT_BYTES)


def _sigmoid(x):
    return 1.0 / (1.0 + jnp.exp(-x))


def _mm_kernel(a_ref, b_ref, o_ref, abf_ref):
    @pl.when(pl.program_id(1) == 0)
    def _():
        abf_ref[...] = a_ref[...].astype(BF16)

    o_ref[...] = jnp.dot(abf_ref[...], b_ref[...], preferred_element_type=F32).astype(o_ref.dtype)


def matmul(a, b, out_dtype=F32, tm=512, tn=1024):
    m, k = a.shape
    _, n = b.shape
    tm = _pick_tile(m, tm, 8)
    tn = _pick_tile(n, tn)
    return pl.pallas_call(
        _mm_kernel,
        grid=(m // tm, n // tn),
        in_specs=[pl.BlockSpec((tm, k), lambda i, j: (i, 0)),
                  pl.BlockSpec((k, tn), lambda i, j: (0, j))],
        out_specs=pl.BlockSpec((tm, tn), lambda i, j: (i, j)),
        out_shape=jax.ShapeDtypeStruct((m, n), out_dtype),
        scratch_shapes=[pltpu.VMEM((tm, k), BF16)],
        compiler_params=_params("parallel", "arbitrary"),
        name="matmul",
    )(a, b)


def _layer_norm_rows(z, g, b):
    mu = jnp.mean(z, axis=-1, keepdims=True)
    zc = z - mu
    var = jnp.mean(zc * zc, axis=-1, keepdims=True)
    return zc * lax.rsqrt(var + NORM_EPS) * g + b


def _res_ln_kernel(h_ref, y_ref, g_ref, b_ref, o_ref, *, alpha):
    o_ref[...] = _layer_norm_rows(alpha * h_ref[...] + y_ref[...], g_ref[...], b_ref[...])


def residual_layer_norm(h, y, g, b, alpha, tm=256):
    n, d = h.shape
    tm = _pick_tile(n, tm, 8)
    row = pl.BlockSpec((tm, d), lambda i: (i, 0))
    vec = pl.BlockSpec((1, d), lambda i: (0, 0))
    return pl.pallas_call(
        functools.partial(_res_ln_kernel, alpha=alpha),
        grid=(n // tm,),
        in_specs=[row, row, vec, vec],
        out_specs=row,
        out_shape=jax.ShapeDtypeStruct((n, d), F32),
        compiler_params=_params("parallel"),
        name="residual_layer_norm",
    )(h, y, g.reshape(1, d), b.reshape(1, d))


def _rope_tab_kernel(pc_ref, pr_ref, f128_ref, f64_ref, fc128_ref, fc64_ref,
                     c128_ref, s128_ref, c64_ref, s64_ref, ct128_ref, st128_ref, ct64_ref, st64_ref):
    pos_col = pc_ref[...]
    pos_row = pr_ref[...]
    lane = lax.broadcasted_iota(jnp.int32, (1, LANES), 1)
    for hd, f_ref, c_ref, s_ref in ((HEAD_DIM, f128_ref, c128_ref, s128_ref),
                                    (IDX_DIM, f64_ref, c64_ref, s64_ref)):
        ang = pos_col * f_ref[...]
        sign = jnp.where((lane % hd) < hd // 2, -1.0, 1.0)
        c_ref[...] = jnp.cos(ang)
        s_ref[...] = jnp.sin(ang) * sign
    for fc_ref, c_ref, s_ref in ((fc128_ref, ct128_ref, st128_ref), (fc64_ref, ct64_ref, st64_ref)):
        ang = fc_ref[...] * pos_row
        c_ref[...] = jnp.cos(ang)
        s_ref[...] = jnp.sin(ang)


def rope_tables(positions):
    n = positions.size
    tm = _pick_tile(n, 512)
    pos = positions.reshape(n).astype(F32)

    def inv_freq(hd):
        return ROPE_THETA ** (-2.0 * jnp.arange(hd // 2, dtype=F32) / hd)

    def lanes(hd):
        return jnp.tile(inv_freq(hd), LANES // (hd // 2)).reshape(1, LANES)

    h128, h64 = HEAD_DIM // 2, IDX_DIM // 2
    tab = pl.BlockSpec((tm, LANES), lambda i: (i, 0))
    vec = pl.BlockSpec((1, LANES), lambda i: (0, 0))
    tok = jax.ShapeDtypeStruct((n, LANES), F32)
    return pl.pallas_call(
        _rope_tab_kernel,
        grid=(n // tm,),
        in_specs=[pl.BlockSpec((tm, 1), lambda i: (i, 0)), pl.BlockSpec((1, tm), lambda i: (0, i)), vec, vec,
                  pl.BlockSpec((h128, 1), lambda i: (0, 0)), pl.BlockSpec((h64, 1), lambda i: (0, 0))],
        out_specs=[tab, tab, tab, tab,
                   pl.BlockSpec((h128, tm), lambda i: (0, i)), pl.BlockSpec((h128, tm), lambda i: (0, i)),
                   pl.BlockSpec((h64, tm), lambda i: (0, i)), pl.BlockSpec((h64, tm), lambda i: (0, i))],
        out_shape=[tok, tok, tok, tok,
                   jax.ShapeDtypeStruct((h128, n), F32), jax.ShapeDtypeStruct((h128, n), F32),
                   jax.ShapeDtypeStruct((h64, n), F32), jax.ShapeDtypeStruct((h64, n), F32)],
        compiler_params=_params("parallel"),
        name="rope_tables",
    )(pos.reshape(n, 1), pos.reshape(1, n), lanes(HEAD_DIM), lanes(IDX_DIM),
      inv_freq(HEAD_DIM).reshape(h128, 1), inv_freq(IDX_DIM).reshape(h64, 1))


def _rotate_half(x, hd):
    if hd == LANES:
        return pltpu.roll(x, LANES // 2, 1)
    lane = lax.broadcasted_iota(jnp.int32, x.shape, 1)
    half = hd // 2
    return jnp.where((lane % hd) < half, pltpu.roll(x, LANES - half, 1), pltpu.roll(x, half, 1))


def _qprep_kernel(cq_ref, g_ref, wt_ref, cos_ref, sin_ref, o_ref, xn_ref, *, hd, scale):
    @pl.when(pl.program_id(1) == 0)
    def _():
        x = cq_ref[...]
        ms = jnp.mean(x * x, axis=-1, keepdims=True)
        xn_ref[...] = (x * lax.rsqrt(ms + NORM_EPS) * g_ref[...]).astype(BF16)

    y = lax.dot_general(wt_ref[...], xn_ref[...], _NT, preferred_element_type=F32)
    cos = cos_ref[...]
    sin = sin_ref[...]
    half = hd // 2
    for hh in range(y.shape[0] // hd):
        lo = hh * hd
        x1 = y[lo:lo + half]
        x2 = y[lo + half:lo + hd]
        o_ref[lo:lo + half, :] = ((x1 * cos - x2 * sin) * scale).astype(o_ref.dtype)
        o_ref[lo + half:lo + hd, :] = ((x2 * cos + x1 * sin) * scale).astype(o_ref.dtype)


def query_projection_t(proj, q_rank, q_norm, w_t, cos_t, sin_t, hd, scale, tm=512, tn=1024):
    n = proj.shape[0]
    nout = w_t.shape[0]
    tm = _pick_tile(n, tm)
    tn = _pick_tile(nout, tn)
    tab = pl.BlockSpec((hd // 2, tm), lambda i, j: (0, i))
    return pl.pallas_call(
        functools.partial(_qprep_kernel, hd=hd, scale=scale),
        grid=(n // tm, nout // tn),
        in_specs=[pl.BlockSpec((tm, q_rank), lambda i, j: (i, 0)),
                  pl.BlockSpec((1, q_rank), lambda i, j: (0, 0)),
                  pl.BlockSpec((tn, q_rank), lambda i, j: (j, 0)),
                  tab, tab],
        out_specs=pl.BlockSpec((tn, tm), lambda i, j: (j, i)),
        out_shape=jax.ShapeDtypeStruct((nout, n), BF16),
        scratch_shapes=[pltpu.VMEM((tm, q_rank), BF16)],
        compiler_params=_params("parallel", "arbitrary"),
        name="query_projection",
    )(proj, q_norm.reshape(1, q_rank), w_t, cos_t, sin_t)


def _kprep_kernel(k_ref, v_ref, t_ref, g_ref, b_ref, c128_ref, s128_ref, c64_ref, s64_ref,
                  ko_ref, vt_ref, ka_ref, kb_ref, wt_ref, *, w_scale):
    cos = c128_ref[...]
    sin = s128_ref[...]
    for c in range(k_ref.shape[1] // LANES):
        sl = slice(c * LANES, (c + 1) * LANES)
        x = k_ref[:, sl]
        ko_ref[:, sl] = (x * cos + _rotate_half(x, HEAD_DIM) * sin).astype(BF16)
        vt_ref[c] = v_ref[:, sl].T.astype(BF16)

    t = t_ref[...]
    lane = lax.broadcasted_iota(jnp.int32, t.shape, 1)
    is_key = lane < IDX_DIM
    mu = jnp.sum(jnp.where(is_key, t, 0.0), axis=-1, keepdims=True) * (1.0 / IDX_DIM)
    tc = jnp.where(is_key, t - mu, 0.0)
    var = jnp.sum(tc * tc, axis=-1, keepdims=True) * (1.0 / IDX_DIM)
    y = tc * lax.rsqrt(var + NORM_EPS) * g_ref[...] + b_ref[...]
    y = y * c64_ref[...] + _rotate_half(y, IDX_DIM) * s64_ref[...]
    ka_ref[...] = y.astype(BF16)
    kb_ref[...] = pltpu.roll(y, IDX_DIM, 1).astype(BF16)
    wt_ref[...] = t.T[IDX_DIM:, :] * w_scale


def key_value_projection(proj, q_rank, kv_dim, kidx_g, kidx_b, tabs, w_scale, tk):
    n = proj.shape[0]
    assert q_rank % kv_dim == 0 and (q_rank + 2 * kv_dim) % LANES == 0
    kb0 = q_rank // kv_dim
    tail_blk = (q_rank + 2 * kv_dim) // LANES
    n_kv = kv_dim // HEAD_DIM
    pad = LANES - IDX_DIM
    g = jnp.pad(kidx_g, (0, pad)).reshape(1, LANES)
    b = jnp.pad(kidx_b, (0, pad)).reshape(1, LANES)
    tab = pl.BlockSpec((tk, LANES), lambda i: (i, 0))
    vec = pl.BlockSpec((1, LANES), lambda i: (0, 0))
    return pl.pallas_call(
        functools.partial(_kprep_kernel, w_scale=w_scale),
        grid=(n // tk,),
        in_specs=[pl.BlockSpec((tk, kv_dim), lambda i: (i, kb0)),
                  pl.BlockSpec((tk, kv_dim), lambda i: (i, kb0 + 1)),
                  pl.BlockSpec((tk, LANES), lambda i: (i, tail_blk)),
                  vec, vec, tab, tab, tab, tab],
        out_specs=[pl.BlockSpec((tk, kv_dim), lambda i: (i, 0)),
                   pl.BlockSpec((None, n_kv, HEAD_DIM, tk), lambda i: (i, 0, 0, 0)),
                   tab, tab,
                   pl.BlockSpec((LANES - IDX_DIM, tk), lambda i: (0, i))],
        out_shape=[jax.ShapeDtypeStruct((n, kv_dim), BF16),
                   jax.ShapeDtypeStruct((n // tk, n_kv, HEAD_DIM, tk), BF16),
                   jax.ShapeDtypeStruct((n, LANES), BF16), jax.ShapeDtypeStruct((n, LANES), BF16),
                   jax.ShapeDtypeStruct((LANES - IDX_DIM, n), F32)],
        compiler_params=_params("parallel"),
        name="key_value_projection",
    )(proj, proj, proj, g, b, *tabs)


def _indexer_kernel(q_ref, w_ref, ka_ref, kb_ref, o_ref, key_ref, cut_ref, *, n_heads, k_sel, seq):
    nk, tk, tq = key_ref.shape
    qi = pl.program_id(1)
    n_live = qi + 1
    key_idx0 = lax.broadcasted_iota(jnp.int32, (tk, tq), 0)
    q_idx = lax.broadcasted_iota(jnp.int32, (tk, tq), 1) + qi * tq

    def score_block(kj, carry):
        off = pl.multiple_of(kj * tk, tk)
        ka = ka_ref[pl.ds(off, tk), :]
        kb = kb_ref[pl.ds(off, tk), :]
        acc = jnp.zeros((tk, tq), F32)
        for p in range(n_heads // 2):
            qp = q_ref[p * LANES:(p + 1) * LANES, :]
            la = jnp.dot(ka, qp, preferred_element_type=F32)
            lb = jnp.dot(kb, qp, preferred_element_type=F32)
            acc = acc + jnp.maximum(la, 0.0) * w_ref[2 * p:2 * p + 1, :]
            acc = acc + jnp.maximum(lb, 0.0) * w_ref[2 * p + 1:2 * p + 2, :]
        s = jnp.where(key_idx0 + kj * tk <= q_idx, acc, -jnp.inf)
        bits = pltpu.bitcast(s, jnp.int32)
        key_ref[kj] = bits ^ ((bits >> 31) & 0x7FFFFFFF)
        return carry

    lax.fori_loop(0, n_live, score_block, 0)

    sub = lax.broadcasted_iota(jnp.int32, (SUBLANES, tq), 0)

    def count(pred):
        def body(kj, c):
            for r in range(tk // SUBLANES):
                key = key_ref[kj, r * SUBLANES:(r + 1) * SUBLANES, :]
                idx = sub + (kj * tk + r * SUBLANES)
                c = c + jnp.where(pred(key, idx), 1.0, 0.0)
            return c
        c = lax.fori_loop(0, n_live, body, jnp.zeros((SUBLANES, tq), F32))
        return jnp.broadcast_to(jnp.sum(c, axis=0, keepdims=True), (SUBLANES, tq))

    def bit_step(i, u):
        cand = u | lax.shift_left(jnp.int32(1), 31 - i)
        cand_signed = cand ^ INT_MIN
        cnt = count(lambda key, idx: key >= cand_signed)
        return jnp.where(cnt >= k_sel, cand, u)

    u = lax.fori_loop(0, 32, bit_step, jnp.zeros((SUBLANES, tq), jnp.int32))
    thr = u ^ INT_MIN

    n_gt = count(lambda key, idx: key > thr)
    n_ge = count(lambda key, idx: key >= thr)

    cut_ref[...] = jnp.full((SUBLANES, tq), seq, jnp.int32)

    @pl.when(jnp.max(n_ge) > k_sel)
    def _():
        need = k_sel - n_gt
        n_bits = max(1, (seq - 1).bit_length())

        def idx_step(i, p):
            t = p + lax.shift_left(jnp.int32(1), n_bits - 1 - i)
            cnt = count(lambda key, idx: (key == thr) & (idx < t))
            return jnp.where(cnt < need, t, p)

        cut_ref[...] = lax.fori_loop(0, n_bits, idx_step, jnp.zeros((SUBLANES, tq), jnp.int32))

    thr_b = jnp.broadcast_to(thr[:1], (tk, tq))
    cut_b = jnp.broadcast_to(cut_ref[:1, :], (tk, tq))

    def emit(kj, carry):
        key = key_ref[kj]
        idx = key_idx0 + kj * tk
        keep = (key > thr_b) | ((key == thr_b) & (idx <= cut_b))
        keep = keep & (idx <= q_idx)
        o_ref[kj] = jnp.where(keep, 0.0, NEG_BIAS).astype(BF16)
        return carry

    lax.fori_loop(0, n_live, emit, 0)

    def emit_dead(kj, carry):
        o_ref[kj] = jnp.full((tk, tq), NEG_BIAS, BF16)
        return carry

    lax.fori_loop(n_live, nk, emit_dead, 0)


def indexer_bias(q_idx_t, w_t, k_a, k_b, batch, seq, n_heads, k_sel, tq):
    nq = seq // tq
    nk = nq
    w_rows = -(-n_heads // SUBLANES) * SUBLANES
    kspec = pl.BlockSpec((seq, LANES), lambda b, i: (b, 0))
    return pl.pallas_call(
        functools.partial(_indexer_kernel, n_heads=n_heads, k_sel=k_sel, seq=seq),
        grid=(batch, nq),
        in_specs=[pl.BlockSpec((n_heads * IDX_DIM, tq), lambda b, i: (0, b * nq + i)),
                  pl.BlockSpec((w_rows, tq), lambda b, i: (0, b * nq + i)),
                  kspec, kspec],
        out_specs=pl.BlockSpec((None, nk, tq, tq), lambda b, i: (b * nq + i, 0, 0, 0)),
        out_shape=jax.ShapeDtypeStruct((batch * nq, nk, tq, tq), BF16),
        scratch_shapes=[pltpu.VMEM((nk, tq, tq), jnp.int32), pltpu.VMEM((SUBLANES, tq), jnp.int32)],
        compiler_params=_params("parallel", "arbitrary"),
        name="indexer_bias",
    )(q_idx_t, w_t, k_a, k_b)


def _attn_kernel(q_ref, k_ref, v_ref, b_ref, o_ref, qcat_ref, sa_ref, sb_ref, m_ref, l_ref, acc_ref):
    _, tk, tq = b_ref.shape
    qi = pl.program_id(2)
    for h in range(Q_GROUP):
        qcat_ref[:, h * tq:(h + 1) * tq] = q_ref[h * HEAD_DIM:(h + 1) * HEAD_DIM, :]
    m_ref[...] = jnp.full(m_ref.shape, NEG_BIAS, F32)
    l_ref[...] = jnp.zeros(l_ref.shape, F32)
    acc_ref[...] = jnp.zeros(acc_ref.shape, F32)

    def scores(kj):
        k = k_ref[pl.ds(pl.multiple_of(kj * tk, tk), tk), :]
        return jnp.dot(k, qcat_ref[...], preferred_element_type=F32)

    def accumulate(kj, s):
        bias = b_ref[kj].astype(F32)
        s = s + jnp.concatenate([bias] * Q_GROUP, axis=1)
        m_prev = m_ref[...]
        m_new = jnp.maximum(m_prev, jnp.max(s, axis=0, keepdims=True))
        alpha = jnp.exp2(m_prev - m_new)
        p = jnp.exp2(s - m_new)
        l_ref[...] = alpha * l_ref[...] + jnp.sum(p, axis=0, keepdims=True)
        acc_ref[...] = alpha * acc_ref[...] + jnp.dot(v_ref[kj], p.astype(BF16), preferred_element_type=F32)
        m_ref[...] = m_new

    n_blocks = qi + 1
    sa_ref[...] = scores(0)

    def pair(i, carry):
        kj = 2 * i
        sb_ref[...] = scores(kj + 1)
        accumulate(kj, sa_ref[...])
        sa_ref[...] = scores(jnp.minimum(kj + 2, qi))
        accumulate(kj + 1, sb_ref[...])
        return carry

    lax.fori_loop(0, n_blocks // 2, pair, 0)

    @pl.when(n_blocks % 2 == 1)
    def _():
        accumulate(qi, sa_ref[...])

    for h in range(Q_GROUP):
        sl = slice(h * tq, (h + 1) * tq)
        o_ref[:, h * HEAD_DIM:(h + 1) * HEAD_DIM] = (acc_ref[:, sl] / l_ref[:, sl]).T.astype(o_ref.dtype)


def masked_attention(q_t, k, v_t, bias, batch, seq, n_kv):
    _, nk, tk, tq = bias.shape
    nq = seq // tq
    n = batch * seq
    gw = Q_GROUP * HEAD_DIM
    v5 = v_t.reshape(batch, nk, n_kv, HEAD_DIM, tk)
    return pl.pallas_call(
        _attn_kernel,
        grid=(batch, n_kv, nq),
        in_specs=[pl.BlockSpec((gw, tq), lambda b, g, i: (g, b * nq + i)),
                  pl.BlockSpec((seq, HEAD_DIM), lambda b, g, i: (b, g)),
                  pl.BlockSpec((None, nk, None, HEAD_DIM, tk), lambda b, g, i: (b, 0, g, 0, 0)),
                  pl.BlockSpec((None, nk, tk, tq), lambda b, g, i: (b * nq + i, 0, 0, 0))],
        out_specs=pl.BlockSpec((tq, gw), lambda b, g, i: (b * nq + i, g)),
        out_shape=jax.ShapeDtypeStruct((n, n_kv * gw), BF16),
        scratch_shapes=[pltpu.VMEM((HEAD_DIM, Q_GROUP * tq), BF16),
                        pltpu.VMEM((tk, Q_GROUP * tq), F32), pltpu.VMEM((tk, Q_GROUP * tq), F32),
                        pltpu.VMEM((1, Q_GROUP * tq), F32), pltpu.VMEM((1, Q_GROUP * tq), F32),
                        pltpu.VMEM((HEAD_DIM, Q_GROUP * tq), F32)],
        compiler_params=_params("parallel", "parallel", "arbitrary"),
        name="masked_attention",
    )(q_t, k, v5, bias)


def _hgrn_kernel(q_ref, f_ref, i_ref, g_ref, lb_ref, on_ref, o_ref, st_ref, intra_ref, mask_ref, *, chunk):
    tb = q_ref.shape[0]
    n_heads = q_ref.shape[1] // HGRN_EXPAND
    c_rows = chunk

    @pl.when(pl.program_id(2) == 0)
    def _():
        st_ref[...] = jnp.zeros(st_ref.shape, F32)
        rr = lax.broadcasted_iota(jnp.int32, mask_ref.shape, 0)
        cc = lax.broadcasted_iota(jnp.int32, mask_ref.shape, 1)
        mask_ref[...] = jnp.where((cc <= rr) & (cc >= rr - rr % c_rows), 1.0, 0.0)

    rowv = lax.broadcasted_iota(jnp.int32, (c_rows, HGRN_EXPAND), 0)
    onorm = on_ref[...]

    def do_chunk(c, carry):
        rows = pl.ds(pl.multiple_of(c * c_rows, c_rows), c_rows)

        heads = []
        for h in range(n_heads):
            sl = slice(h * HGRN_EXPAND, (h + 1) * HGRN_EXPAND)
            qr = q_ref[rows, sl]
            fr = f_ref[rows, sl]
            lb = lb_ref[:, sl]
            q = qr * _sigmoid(qr)
            e = jnp.exp(-jnp.abs(fr))
            r = 1.0 / (1.0 + e)
            sig_pos = jnp.where(fr >= 0, r, e * r)
            sig_neg = jnp.where(fr >= 0, e * r, r)
            log_f = jnp.log(lb + (1.0 - lb) * sig_pos)
            kk = (1.0 - lb) * sig_neg
            b = log_f
            s = 1
            while s < c_rows:
                b = b + jnp.where(rowv >= s, pltpu.roll(b, s, 0), 0.0)
                s *= 2
            b_last = b[c_rows - 1:c_rows, :]
            heads.append(dict(sl=sl, q=q, kk=kk, b=b, b_last=b_last,
                              qt=(q * jnp.exp(b)).astype(BF16), v=i_ref[rows, sl].astype(BF16)))

        b_min = heads[0]["b_last"]
        for hd in heads[1:]:
            b_min = jnp.minimum(b_min, hd["b_last"])
        safe = jnp.min(b_min) >= -SAFE_DECAY_EXPONENT

        @pl.when(safe)
        def _():
            qt_all = jnp.concatenate([hd["qt"] for hd in heads], axis=0)
            kh_all = jnp.concatenate([(hd["kk"] * jnp.exp(-hd["b"])).astype(BF16) for hd in heads], axis=0)
            v_all = jnp.concatenate([hd["v"] for hd in heads], axis=0)
            a = lax.dot_general(qt_all, kh_all, _NT, preferred_element_type=F32)
            a = jnp.where(mask_ref[...] != 0.0, a, 0.0).astype(BF16)
            o_all = jnp.dot(a, v_all, preferred_element_type=F32)
            for h in range(n_heads):
                intra_ref[h] = o_all[h * c_rows:(h + 1) * c_rows]

        @pl.when(jnp.logical_not(safe))
        def _():
            for h, hd in enumerate(heads):
                q, kk, b = hd["q"], hd["kk"], hd["b"]
                v = i_ref[rows, hd["sl"]]

                def off_step(d, o):
                    kd = pltpu.roll(kk, d, 0)
                    bd = pltpu.roll(b, d, 0)
                    vd = pltpu.roll(v, d, 0)
                    wgt = q * kd * jnp.exp(jnp.minimum(b - bd, 0.0))
                    a = jnp.sum(wgt, axis=1, keepdims=True)
                    return o + jnp.where(rowv >= d, a, 0.0) * vd
                intra_ref[h] = lax.fori_loop(0, c_rows, off_step, jnp.zeros((c_rows, HGRN_EXPAND), F32))

        for h, hd in enumerate(heads):
            st = st_ref[h]
            o = intra_ref[h] + lax.dot_general(hd["qt"], st.astype(BF16), _NT, preferred_element_type=F32)
            k_new = (hd["kk"] * jnp.exp(hd["b_last"] - hd["b"])).astype(BF16)
            st_ref[h] = st * jnp.exp(hd["b_last"]) + lax.dot_general(hd["v"], k_new, _TN,
                                                                      preferred_element_type=F32)
            gr = g_ref[rows, hd["sl"]]
            ms = jnp.mean(o * o, axis=-1, keepdims=True)
            out = o * lax.rsqrt(ms + NORM_EPS) * onorm * (gr * _sigmoid(gr))
            o_ref[rows, hd["sl"]] = out.astype(o_ref.dtype)
        return carry

    lax.fori_loop(0, tb // c_rows, do_chunk, 0)


def hgrn_recurrence(proj, lower_bound, out_norm, batch, seq, d_model, tb=512, heads_per_step=8):
    n = proj.shape[0]
    tb = min(tb, seq)
    nt = seq // tb
    width = min(heads_per_step * HGRN_EXPAND, d_model)
    ng = d_model // width
    chunk = min(HGRN_CHUNK, tb)

    def part(p):
        return pl.BlockSpec((tb, width), lambda b, g, t: (b * nt + t, p * ng + g))

    return pl.pallas_call(
        functools.partial(_hgrn_kernel, chunk=chunk),
        grid=(batch, ng, nt),
        in_specs=[part(0), part(1), part(2), part(3),
                  pl.BlockSpec((1, width), lambda b, g, t: (0, g)),
                  pl.BlockSpec((1, HGRN_EXPAND), lambda b, g, t: (0, 0))],
        out_specs=pl.BlockSpec((tb, width), lambda b, g, t: (b * nt + t, g)),
        out_shape=jax.ShapeDtypeStruct((n, d_model), BF16),
        scratch_shapes=[pltpu.VMEM((width // HGRN_EXPAND, HGRN_EXPAND, HGRN_EXPAND), F32),
                        pltpu.VMEM((width // HGRN_EXPAND, chunk, HGRN_EXPAND), F32),
                        pltpu.VMEM((width // HGRN_EXPAND * chunk, width // HGRN_EXPAND * chunk), F32)],
        compiler_params=_params("parallel", "parallel", "arbitrary"),
        name="hgrn_recurrence",
    )(proj, proj, proj, proj, lower_bound.reshape(1, d_model), out_norm.reshape(1, HGRN_EXPAND))


def _router_kernel(h_ref, rw_ref, rb_ref, o_ref):
    logits = lax.dot_general(rw_ref[...], h_ref[...].astype(BF16), _NT, preferred_element_type=F32)
    s_all = _sigmoid(logits)
    r_all = s_all + rb_ref[...]
    tm = logits.shape[1]

    def rows(x, g):
        return [x[g * EXPERTS_PER_GROUP + j:g * EXPERTS_PER_GROUP + j + 1, :] for j in range(EXPERTS_PER_GROUP)]

    best_score = None
    for g in range(N_GROUPS):
        a, b, c, d = rows(r_all, g)
        hi_ab, lo_ab = jnp.maximum(a, b), jnp.minimum(a, b)
        hi_cd, lo_cd = jnp.maximum(c, d), jnp.minimum(c, d)
        top1 = jnp.maximum(hi_ab, hi_cd)
        top2 = jnp.maximum(jnp.minimum(hi_ab, hi_cd), jnp.maximum(lo_ab, lo_cd))
        score = top1 + top2
        if g == 0:
            best_score, best = score, jnp.zeros((1, tm), F32)
            best_r, best_s = rows(r_all, 0), rows(s_all, 0)
        else:
            upd = score > best_score
            best_score = jnp.where(upd, score, best_score)
            best = jnp.where(upd, float(g), best)
            best_r = [jnp.where(upd, n, o) for n, o in zip(rows(r_all, g), best_r)]
            best_s = [jnp.where(upd, n, o) for n, o in zip(rows(s_all, g), best_s)]

    picked = []
    for j in range(EXPERTS_PER_GROUP):
        rank = jnp.zeros((1, tm), F32)
        for i in range(EXPERTS_PER_GROUP):
            if i == j:
                continue
            ahead = (best_r[i] > best_r[j]) | ((best_r[i] == best_r[j]) & (i < j))
            rank = rank + jnp.where(ahead, 1.0, 0.0)
        picked.append(jnp.where(rank < 2.0, best_s[j], 0.0))
    total = picked[0] + picked[1] + picked[2] + picked[3]
    out_rows = [p / total for p in picked] + [best] + [jnp.zeros((1, tm), F32)] * 3
    o_ref[...] = jnp.concatenate(out_rows, axis=0)


def route(h, router_w_t, router_b, tm=512):
    n, d = h.shape
    tm = _pick_tile(n, tm)
    return pl.pallas_call(
        _router_kernel,
        grid=(n // tm,),
        in_specs=[pl.BlockSpec((tm, d), lambda i: (i, 0)),
                  pl.BlockSpec((N_EXPERTS, d), lambda i: (0, 0)),
                  pl.BlockSpec((N_EXPERTS, 1), lambda i: (0, 0))],
        out_specs=pl.BlockSpec((8, tm), lambda i: (0, i)),
        out_shape=jax.ShapeDtypeStruct((8, n), F32),
        compiler_params=_params("parallel"),
        name="route",
    )(h, router_w_t, router_b.reshape(N_EXPERTS, 1))


def _moe_kernel(grp_ref, cnt_ref, tok_ref, x_hbm, gate_ref, wg_ref, wu_ref, wd_ref, lng_ref, lnb_ref,
                out_hbm, xg_ref, xb_ref, acc_ref, sem_in, sem_out, *, n_tokens, alpha):
    del grp_ref
    tm = xg_ref.shape[0]
    blk = pl.program_id(0)
    e = pl.program_id(1)
    n_valid = cnt_ref[blk]
    active = n_valid > 0
    base = blk * tm

    @pl.when(active & (e == 0))
    def _():
        def issue(r, carry):
            src = jnp.minimum(tok_ref[base + r], n_tokens - 1)
            pltpu.make_async_copy(x_hbm.at[pl.ds(src, 1)], xg_ref.at[pl.ds(r, 1)], sem_in).start()
            return carry
        lax.fori_loop(0, tm, issue, 0)
        pltpu.make_async_copy(x_hbm.at[pl.ds(0, tm)], xg_ref, sem_in).wait()
        xb_ref[...] = xg_ref[...].astype(BF16)
        acc_ref[...] = jnp.zeros(acc_ref.shape, F32)

    @pl.when(active)
    def _():
        x = xb_ref[...]
        hg = jnp.dot(x, wg_ref[...], preferred_element_type=F32)
        hu = jnp.dot(x, wu_ref[...], preferred_element_type=F32)
        hm = (hg * _sigmoid(hg) * hu).astype(BF16)
        y = jnp.dot(hm, wd_ref[...], preferred_element_type=F32)
        acc_ref[...] += y * gate_ref[...]

    @pl.when(active & (e == EXPERTS_PER_GROUP - 1))
    def _():
        acc_ref[...] = _layer_norm_rows(alpha * xg_ref[...] + acc_ref[...], lng_ref[...], lnb_ref[...])

        def put(r, carry):
            tok = tok_ref[base + r]
            pltpu.make_async_copy(acc_ref.at[pl.ds(r, 1)], out_hbm.at[pl.ds(tok, 1)], sem_out).start()
            return carry
        lax.fori_loop(0, n_valid, put, 0)

        def drain(r, carry):
            pltpu.make_async_copy(acc_ref.at[pl.ds(0, 1)], out_hbm.at[pl.ds(0, 1)], sem_out).wait()
            return carry
        lax.fori_loop(0, n_valid, drain, 0)


def moe_ffn_layer_norm(h, routing, wg, wu, wd, layer, ln_g, ln_b, alpha, tm=512):
    n, d = h.shape
    tm = min(tm, n)
    gates = routing[:EXPERTS_PER_GROUP]
    grp = routing[EXPERTS_PER_GROUP].astype(jnp.int32)

    onehot = (grp[:, None] == jnp.arange(N_GROUPS, dtype=jnp.int32)[None, :]).astype(jnp.int32)
    csum = jnp.cumsum(onehot, axis=0)
    counts = csum[-1]
    rank = jnp.sum(csum * onehot, axis=1) - 1
    padded = (counts + tm - 1) // tm * tm
    pends = jnp.cumsum(padded)
    dest = jnp.sum(onehot * (pends - padded)[None, :], axis=1) + rank
    p_rows = n + N_GROUPS * tm
    nb = p_rows // tm
    payload = jnp.concatenate([gates.T, jnp.arange(n, dtype=F32)[:, None]], axis=1)
    empty = jnp.zeros((p_rows, EXPERTS_PER_GROUP + 1), F32).at[:, EXPERTS_PER_GROUP].set(float(n))
    placed = empty.at[dest].set(payload)
    row_tok = placed[:, EXPERTS_PER_GROUP].astype(jnp.int32)
    gate_sorted = placed[:, :EXPERTS_PER_GROUP].T.reshape(EXPERTS_PER_GROUP, p_rows, 1)
    blk_start = jnp.arange(nb, dtype=jnp.int32) * tm
    blk_grp = jnp.minimum(jnp.searchsorted(pends, blk_start, side='right'), N_GROUPS - 1).astype(jnp.int32)
    grp_end = (pends - padded + counts)[blk_grp]
    blk_cnt = jnp.where(blk_start < pends[-1], jnp.clip(grp_end - blk_start, 0, tm), 0).astype(jnp.int32)

    d_exp = wg.shape[-1]
    last = EXPERTS_PER_GROUP - 1

    def w_idx(blk, e, grp_ref, cnt_ref, tok_ref):
        act = jnp.minimum(cnt_ref[blk], 1)
        return layer * N_EXPERTS + grp_ref[blk] * EXPERTS_PER_GROUP + e * act + last * (1 - act)

    grid_spec = pltpu.PrefetchScalarGridSpec(
        num_scalar_prefetch=3,
        grid=(nb, EXPERTS_PER_GROUP),
        in_specs=[pl.BlockSpec(memory_space=pl.ANY),
                  pl.BlockSpec((None, tm, 1), lambda blk, e, *_: (e, blk, 0)),
                  pl.BlockSpec((None, d, d_exp), lambda blk, e, *s: (w_idx(blk, e, *s), 0, 0)),
                  pl.BlockSpec((None, d, d_exp), lambda blk, e, *s: (w_idx(blk, e, *s), 0, 0)),
                  pl.BlockSpec((None, d_exp, d), lambda blk, e, *s: (w_idx(blk, e, *s), 0, 0)),
                  pl.BlockSpec((1, d), lambda blk, e, *_: (0, 0)),
                  pl.BlockSpec((1, d), lambda blk, e, *_: (0, 0))],
        out_specs=pl.BlockSpec(memory_space=pl.ANY),
        scratch_shapes=[pltpu.VMEM((tm, d), F32), pltpu.VMEM((tm, d), BF16), pltpu.VMEM((tm, d), F32),
                        pltpu.SemaphoreType.DMA(()), pltpu.SemaphoreType.DMA(())],
    )
    return pl.pallas_call(
        functools.partial(_moe_kernel, n_tokens=n, alpha=alpha),
        grid_spec=grid_spec,
        out_shape=jax.ShapeDtypeStruct((n, d), F32),
        compiler_params=_params("arbitrary", "arbitrary"),
        name="moe_ffn_layer_norm",
    )(blk_grp, blk_cnt, row_tok, h, gate_sorted, wg, wu, wd, ln_g.reshape(1, d), ln_b.reshape(1, d))


def kernel(x, positions, attn_w_in, attn_q_norm, attn_w_uq, attn_w_uq_idx, attn_kidx_norm_g, attn_kidx_norm_b, attn_w_o, hgrn_w_in, hgrn_lower_bounds, hgrn_out_norm, hgrn_w_o, ln_mix_g, ln_mix_b, ln_ffn_g, ln_ffn_b, router_w, router_b, moe_w_gate, moe_w_up, moe_w_down):
    batch, seq, d = x.shape
    n = batch * seq
    depth = ln_mix_g.shape[0]
    alpha = (2 * depth) ** 0.25
    q_rank = attn_w_uq.shape[1]
    n_q = attn_w_uq.shape[2] // HEAD_DIM
    n_kv = n_q // Q_GROUP
    kv_dim = n_kv * HEAD_DIM
    n_idx = attn_w_uq_idx.shape[2] // IDX_DIM
    k_sel = min(INDEX_TOPK, seq // 4)
    att_in = attn_w_in.shape[2]
    att_pad = q_rank + 2 * kv_dim + LANES - att_in
    tk = min(ATTN_BLOCK, seq)

    w_in_a = jnp.pad(attn_w_in, ((0, 0), (0, 0), (0, att_pad))).astype(BF16)
    w_uq_t = attn_w_uq.transpose(0, 2, 1).astype(BF16)
    w_uq_idx_t = attn_w_uq_idx.transpose(0, 2, 1).astype(BF16)
    w_o_a = attn_w_o.astype(BF16)
    w_in_h = hgrn_w_in.astype(BF16)
    w_o_h = hgrn_w_o.astype(BF16)
    d_exp = moe_w_gate.shape[-1]
    wg = moe_w_gate.astype(BF16).reshape(depth * N_EXPERTS, d, d_exp)
    wu = moe_w_up.astype(BF16).reshape(depth * N_EXPERTS, d, d_exp)
    wd = moe_w_down.astype(BF16).reshape(depth * N_EXPERTS, d_exp, d)
    router_w_t = router_w.T.astype(BF16)
    lb_soft = jax.nn.softmax(hgrn_lower_bounds.astype(F32), axis=0)
    lower_bounds = jnp.cumsum(lb_soft, axis=0) - lb_soft[0]

    c128, s128, c64, s64, ct128, st128, ct64, st64 = rope_tables(positions)

    h = x.reshape(n, d)
    for layer in range(depth):
        j = layer // 2
        if layer % 2 == 0:
            proj = matmul(h, w_in_a[j], tn=640)
            q_t = query_projection_t(proj, q_rank, attn_q_norm[j], w_uq_t[j], ct128, st128, HEAD_DIM,
                                     HEAD_DIM ** -0.5 * LOG2E)
            q_idx_t = query_projection_t(proj, q_rank, attn_q_norm[j], w_uq_idx_t[j], ct64, st64, IDX_DIM, 1.0)
            k, v_t, k_a, k_b, w_t = key_value_projection(
                proj, q_rank, kv_dim, attn_kidx_norm_g[j], attn_kidx_norm_b[j], (c128, s128, c64, s64),
                n_idx ** -0.5 * IDX_DIM ** -0.5, tk)
            bias = indexer_bias(q_idx_t, w_t, k_a, k_b, batch, seq, n_idx, k_sel, tk)
            o = masked_attention(q_t, k, v_t, bias, batch, seq, n_kv)
            mix = matmul(o, w_o_a[j])
        else:
            proj = matmul(h, w_in_h[j])
            o = hgrn_recurrence(proj, lower_bounds[layer], hgrn_out_norm[j], batch, seq, d)
            mix = matmul(o, w_o_h[j])
        h = residual_layer_norm(h, mix, ln_mix_g[layer], ln_mix_b[layer], alpha)
        routing = route(h, router_w_t, router_b)
        h = moe_ffn_layer_norm(h, routing, wg, wu, wd, layer, ln_ffn_g[layer], ln_ffn_b[layer], alpha)
    return h.reshape(batch, seq, d)
```

```python
import functools
import math

import jax
import jax.numpy as jnp
from jax import lax
from jax.experimental import pallas as pl
from jax.experimental.pallas import tpu as pltpu

F32 = jnp.float32
BF16 = jnp.bfloat16

HEAD_DIM = 128
Q_GROUP = 4
IDX_DIM = 64
INDEX_TOPK = 256
ROPE_THETA = 10000.0
HGRN_EXPAND = 128
HGRN_CHUNK = 64
ATTN_BLOCK = 256
N_EXPERTS = 32
N_GROUPS = 8
EXPERTS_PER_GROUP = N_EXPERTS // N_GROUPS
MOE_COL_CHUNK = 1024
NORM_EPS = 1e-5

LANES = 128
SUBLANES = 8
VMEM_LIMIT_BYTES = 56 * 1024 * 1024

NEG_BIAS = -1e30
INT_MIN = -(2 ** 31)
LOG2E = 1.4426950408889634
SAFE_DECAY_EXPONENT = 80.0

_NT = (((1,), (1,)), ((), ()))
_TN = (((0,), (0,)), ((), ()))


def _pick_tile(n, target, quantum=LANES):
    if n <= target:
        return n
    t = (target // quantum) * quantum
    while t > quantum and n % t:
        t -= quantum
    assert n % t == 0, (n, target)
    return t


def _params(*sem):
    return pltpu.CompilerParams(dimension_semantics=sem, vmem_limit_bytes=VMEM_LIMIT_BYTES)


def _sigmoid(x):
    return 1.0 / (1.0 + jnp.exp(-x))


def _mm_kernel(a_ref, b_ref, o_ref, abf_ref):
    @pl.when(pl.program_id(1) == 0)
    def _():
        abf_ref[...] = a_ref[...].astype(BF16)

    o_ref[...] = jnp.dot(abf_ref[...], b_ref[...], preferred_element_type=F32).astype(o_ref.dtype)


def matmul(a, b, out_dtype=F32, tm=512, tn=1024):
    m, k = a.shape
    _, n = b.shape
    tm = _pick_tile(m, tm, 8)
    tn = _pick_tile(n, tn)
    return pl.pallas_call(
        _mm_kernel,
        grid=(m // tm, n // tn),
        in_specs=[pl.BlockSpec((tm, k), lambda i, j: (i, 0)),
                  pl.BlockSpec((k, tn), lambda i, j: (0, j))],
        out_specs=pl.BlockSpec((tm, tn), lambda i, j: (i, j)),
        out_shape=jax.ShapeDtypeStruct((m, n), out_dtype),
        scratch_shapes=[pltpu.VMEM((tm, k), BF16)],
        compiler_params=_params("parallel", "arbitrary"),
        name="matmul",
    )(a, b)


def _layer_norm_rows(z, g, b):
    mu = jnp.mean(z, axis=-1, keepdims=True)
    zc = z - mu
    var = jnp.mean(zc * zc, axis=-1, keepdims=True)
    return zc * lax.rsqrt(var + NORM_EPS) * g + b


LN_ROWS = 128


def _mm_res_ln_kernel(a_ref, w_ref, h_ref, g_ref, b_ref, o_ref, *, alpha):
    k = pl.program_id(1)

    @pl.when(k == 0)
    def _():
        o_ref[...] = jnp.zeros(o_ref.shape, F32)

    o_ref[...] += jnp.dot(a_ref[...], w_ref[...], preferred_element_type=F32)

    @pl.when(k == pl.num_programs(1) - 1)
    def _():
        rows = min(LN_ROWS, o_ref.shape[0])

        def norm(r, carry):
            sl = pl.ds(pl.multiple_of(r * rows, rows), rows)
            o_ref[sl, :] = _layer_norm_rows(alpha * h_ref[sl, :] + o_ref[sl, :], g_ref[...], b_ref[...])
            return carry
        lax.fori_loop(0, o_ref.shape[0] // rows, norm, 0)


def matmul_residual_layer_norm(a, w, h, g, b, alpha, tm=512, tk=512):
    m, k = a.shape
    d = w.shape[1]
    tm = _pick_tile(m, tm, 8)
    tk = _pick_tile(k, tk)
    row = pl.BlockSpec((tm, d), lambda i, j: (i, 0))
    vec = pl.BlockSpec((1, d), lambda i, j: (0, 0))
    return pl.pallas_call(
        functools.partial(_mm_res_ln_kernel, alpha=alpha),
        grid=(m // tm, k // tk),
        in_specs=[pl.BlockSpec((tm, tk), lambda i, j: (i, j)),
                  pl.BlockSpec((tk, d), lambda i, j: (j, 0)),
                  pl.BlockSpec((tm, d), lambda i, j: (i, 0), pipeline_mode=pl.Buffered(1)),
                  vec, vec],
        out_specs=row,
        out_shape=jax.ShapeDtypeStruct((m, d), F32),
        compiler_params=_params("parallel", "arbitrary"),
        name="matmul_residual_layer_norm",
    )(a, w, h, g.reshape(1, d), b.reshape(1, d))


def _rope_tab_kernel(pc_ref, pr_ref, f128_ref, f64_ref, fc128_ref, fc64_ref,
                     c128_ref, s128_ref, c64_ref, s64_ref, ct128_ref, st128_ref, ct64_ref, st64_ref):
    pos_col = pc_ref[...]
    pos_row = pr_ref[...]
    lane = lax.broadcasted_iota(jnp.int32, (1, LANES), 1)
    for hd, f_ref, c_ref, s_ref in ((HEAD_DIM, f128_ref, c128_ref, s128_ref),
                                    (IDX_DIM, f64_ref, c64_ref, s64_ref)):
        ang = pos_col * f_ref[...]
        sign = jnp.where((lane % hd) < hd // 2, -1.0, 1.0)
        c_ref[...] = jnp.cos(ang)
        s_ref[...] = jnp.sin(ang) * sign
    for fc_ref, c_ref, s_ref in ((fc128_ref, ct128_ref, st128_ref), (fc64_ref, ct64_ref, st64_ref)):
        ang = fc_ref[...] * pos_row
        c_ref[...] = jnp.cos(ang)
        s_ref[...] = jnp.sin(ang)


def rope_tables(positions):
    n = positions.size
    tm = _pick_tile(n, 512)
    pos = positions.reshape(n).astype(F32)

    def inv_freq(hd):
        return ROPE_THETA ** (-2.0 * jnp.arange(hd // 2, dtype=F32) / hd)

    def lanes(hd):
        return jnp.tile(inv_freq(hd), LANES // (hd // 2)).reshape(1, LANES)

    h128, h64 = HEAD_DIM // 2, IDX_DIM // 2
    tab = pl.BlockSpec((tm, LANES), lambda i: (i, 0))
    vec = pl.BlockSpec((1, LANES), lambda i: (0, 0))
    tok = jax.ShapeDtypeStruct((n, LANES), F32)
    return pl.pallas_call(
        _rope_tab_kernel,
        grid=(n // tm,),
        in_specs=[pl.BlockSpec((tm, 1), lambda i: (i, 0)), pl.BlockSpec((1, tm), lambda i: (0, i)), vec, vec,
                  pl.BlockSpec((h128, 1), lambda i: (0, 0)), pl.BlockSpec((h64, 1), lambda i: (0, 0))],
        out_specs=[tab, tab, tab, tab,
                   pl.BlockSpec((h128, tm), lambda i: (0, i)), pl.BlockSpec((h128, tm), lambda i: (0, i)),
                   pl.BlockSpec((h64, tm), lambda i: (0, i)), pl.BlockSpec((h64, tm), lambda i: (0, i))],
        out_shape=[tok, tok, tok, tok,
                   jax.ShapeDtypeStruct((h128, n), F32), jax.ShapeDtypeStruct((h128, n), F32),
                   jax.ShapeDtypeStruct((h64, n), F32), jax.ShapeDtypeStruct((h64, n), F32)],
        compiler_params=_params("parallel"),
        name="rope_tables",
    )(pos.reshape(n, 1), pos.reshape(1, n), lanes(HEAD_DIM), lanes(IDX_DIM),
      inv_freq(HEAD_DIM).reshape(h128, 1), inv_freq(IDX_DIM).reshape(h64, 1))


def _rotate_half(x, hd):
    if hd == LANES:
        return pltpu.roll(x, LANES // 2, 1)
    lane = lax.broadcasted_iota(jnp.int32, x.shape, 1)
    half = hd // 2
    return jnp.where((lane % hd) < half, pltpu.roll(x, LANES - half, 1), pltpu.roll(x, half, 1))


def _qprep_kernel(cq_ref, g_ref, wt_ref, cos_ref, sin_ref, o_ref, xn_ref, *, hd, scale):
    @pl.when(pl.program_id(1) == 0)
    def _():
        x = cq_ref[...]
        ms = jnp.mean(x * x, axis=-1, keepdims=True)
        xn_ref[...] = (x * lax.rsqrt(ms + NORM_EPS) * g_ref[...]).astype(BF16)

    y = lax.dot_general(wt_ref[...], xn_ref[...], _NT, preferred_element_type=F32)
    cos = cos_ref[...]
    sin = sin_ref[...]
    half = hd // 2
    for hh in range(y.shape[0] // hd):
        lo = hh * hd
        x1 = y[lo:lo + half]
        x2 = y[lo + half:lo + hd]
        o_ref[lo:lo + half, :] = ((x1 * cos - x2 * sin) * scale).astype(o_ref.dtype)
        o_ref[lo + half:lo + hd, :] = ((x2 * cos + x1 * sin) * scale).astype(o_ref.dtype)


def query_projection_t(proj, q_rank, q_norm, w_t, cos_t, sin_t, hd, scale, tm=512, tn=1024):
    n = proj.shape[0]
    nout = w_t.shape[0]
    tm = _pick_tile(n, tm)
    tn = _pick_tile(nout, tn)
    tab = pl.BlockSpec((hd // 2, tm), lambda i, j: (0, i))
    return pl.pallas_call(
        functools.partial(_qprep_kernel, hd=hd, scale=scale),
        grid=(n // tm, nout // tn),
        in_specs=[pl.BlockSpec((tm, q_rank), lambda i, j: (i, 0)),
                  pl.BlockSpec((1, q_rank), lambda i, j: (0, 0)),
                  pl.BlockSpec((tn, q_rank), lambda i, j: (j, 0)),
                  tab, tab],
        out_specs=pl.BlockSpec((tn, tm), lambda i, j: (j, i)),
        out_shape=jax.ShapeDtypeStruct((nout, n), BF16),
        scratch_shapes=[pltpu.VMEM((tm, q_rank), BF16)],
        compiler_params=_params("parallel", "arbitrary"),
        name="query_projection",
    )(proj, q_norm.reshape(1, q_rank), w_t, cos_t, sin_t)


def _kprep_kernel(k_ref, v_ref, t_ref, g_ref, b_ref, c128_ref, s128_ref, c64_ref, s64_ref,
                  ko_ref, vt_ref, ka_ref, kb_ref, wt_ref, *, w_scale):
    cos = c128_ref[...]
    sin = s128_ref[...]
    for c in range(k_ref.shape[1] // LANES):
        sl = slice(c * LANES, (c + 1) * LANES)
        x = k_ref[:, sl]
        ko_ref[:, sl] = (x * cos + _rotate_half(x, HEAD_DIM) * sin).astype(BF16)
        vt_ref[c] = v_ref[:, sl].T.astype(BF16)

    t = t_ref[...]
    lane = lax.broadcasted_iota(jnp.int32, t.shape, 1)
    is_key = lane < IDX_DIM
    mu = jnp.sum(jnp.where(is_key, t, 0.0), axis=-1, keepdims=True) * (1.0 / IDX_DIM)
    tc = jnp.where(is_key, t - mu, 0.0)
    var = jnp.sum(tc * tc, axis=-1, keepdims=True) * (1.0 / IDX_DIM)
    y = tc * lax.rsqrt(var + NORM_EPS) * g_ref[...] + b_ref[...]
    y = y * c64_ref[...] + _rotate_half(y, IDX_DIM) * s64_ref[...]
    ka_ref[...] = y.astype(BF16)
    kb_ref[...] = pltpu.roll(y, IDX_DIM, 1).astype(BF16)
    wt_ref[...] = t.T[IDX_DIM:, :] * w_scale


def key_value_projection(proj, q_rank, kv_dim, kidx_g, kidx_b, tabs, w_scale, tk):
    n = proj.shape[0]
    assert q_rank % kv_dim == 0 and (q_rank + 2 * kv_dim) % LANES == 0
    kb0 = q_rank // kv_dim
    tail_blk = (q_rank + 2 * kv_dim) // LANES
    n_kv = kv_dim // HEAD_DIM
    pad = LANES - IDX_DIM
    g = jnp.pad(kidx_g, (0, pad)).reshape(1, LANES)
    b = jnp.pad(kidx_b, (0, pad)).reshape(1, LANES)
    tab = pl.BlockSpec((tk, LANES), lambda i: (i, 0))
    vec = pl.BlockSpec((1, LANES), lambda i: (0, 0))
    return pl.pallas_call(
        functools.partial(_kprep_kernel, w_scale=w_scale),
        grid=(n // tk,),
        in_specs=[pl.BlockSpec((tk, kv_dim), lambda i: (i, kb0)),
                  pl.BlockSpec((tk, kv_dim), lambda i: (i, kb0 + 1)),
                  pl.BlockSpec((tk, LANES), lambda i: (i, tail_blk)),
                  vec, vec, tab, tab, tab, tab],
        out_specs=[pl.BlockSpec((tk, kv_dim), lambda i: (i, 0)),
                   pl.BlockSpec((None, n_kv, HEAD_DIM, tk), lambda i: (i, 0, 0, 0)),
                   tab, tab,
                   pl.BlockSpec((LANES - IDX_DIM, tk), lambda i: (0, i))],
        out_shape=[jax.ShapeDtypeStruct((n, kv_dim), BF16),
                   jax.ShapeDtypeStruct((n // tk, n_kv, HEAD_DIM, tk), BF16),
                   jax.ShapeDtypeStruct((n, LANES), BF16), jax.ShapeDtypeStruct((n, LANES), BF16),
                   jax.ShapeDtypeStruct((LANES - IDX_DIM, n), F32)],
        compiler_params=_params("parallel"),
        name="key_value_projection",
    )(proj, proj, proj, g, b, *tabs)


def _indexer_kernel(q_ref, w_ref, ka_ref, kb_ref, o_ref, key_ref, cut_ref, *, n_heads, k_sel, seq):
    nk, tk, tq = key_ref.shape
    qi = pl.program_id(1)
    n_live = qi + 1
    key_idx0 = lax.broadcasted_iota(jnp.int32, (tk, tq), 0)
    q_idx = lax.broadcasted_iota(jnp.int32, (tk, tq), 1) + qi * tq

    def score_block(kj, carry):
        off = pl.multiple_of(kj * tk, tk)
        ka = ka_ref[pl.ds(off, tk), :]
        kb = kb_ref[pl.ds(off, tk), :]
        acc = jnp.zeros((tk, tq), F32)
        for p in range(n_heads // 2):
            qp = q_ref[p * LANES:(p + 1) * LANES, :]
            la = jnp.dot(ka, qp, preferred_element_type=F32)
            lb = jnp.dot(kb, qp, preferred_element_type=F32)
            acc = acc + jnp.maximum(la, 0.0) * w_ref[2 * p:2 * p + 1, :]
            acc = acc + jnp.maximum(lb, 0.0) * w_ref[2 * p + 1:2 * p + 2, :]
        s = jnp.where(key_idx0 + kj * tk <= q_idx, acc, -jnp.inf)
        bits = pltpu.bitcast(s, jnp.int32)
        key_ref[kj] = bits ^ ((bits >> 31) & 0x7FFFFFFF)
        return carry

    lax.fori_loop(0, n_live, score_block, 0)

    sub = lax.broadcasted_iota(jnp.int32, (SUBLANES, tq), 0)

    def count(pred):
        def body(kj, c):
            for r in range(tk // SUBLANES):
                key = key_ref[kj, r * SUBLANES:(r + 1) * SUBLANES, :]
                idx = sub + (kj * tk + r * SUBLANES)
                c = c + jnp.where(pred(key, idx), 1.0, 0.0)
            return c
        c = lax.fori_loop(0, n_live, body, jnp.zeros((SUBLANES, tq), F32))
        return jnp.broadcast_to(jnp.sum(c, axis=0, keepdims=True), (SUBLANES, tq))

    def bit_step(i, u):
        cand = u | lax.shift_left(jnp.int32(1), 31 - i)
        cand_signed = cand ^ INT_MIN
        cnt = count(lambda key, idx: key >= cand_signed)
        return jnp.where(cnt >= k_sel, cand, u)

    u = lax.fori_loop(0, 32, bit_step, jnp.zeros((SUBLANES, tq), jnp.int32))
    thr = u ^ INT_MIN

    n_gt = count(lambda key, idx: key > thr)
    n_ge = count(lambda key, idx: key >= thr)

    cut_ref[...] = jnp.full((SUBLANES, tq), seq, jnp.int32)

    @pl.when(jnp.max(n_ge) > k_sel)
    def _():
        need = k_sel - n_gt
        n_bits = max(1, (seq - 1).bit_length())

        def idx_step(i, p):
            t = p + lax.shift_left(jnp.int32(1), n_bits - 1 - i)
            cnt = count(lambda key, idx: (key == thr) & (idx < t))
            return jnp.where(cnt < need, t, p)

        cut_ref[...] = lax.fori_loop(0, n_bits, idx_step, jnp.zeros((SUBLANES, tq), jnp.int32))

    thr_b = jnp.broadcast_to(thr[:1], (tk, tq))
    cut_b = jnp.broadcast_to(cut_ref[:1, :], (tk, tq))

    def emit(kj, carry):
        key = key_ref[kj]
        idx = key_idx0 + kj * tk
        keep = (key > thr_b) | ((key == thr_b) & (idx <= cut_b))
        keep = keep & (idx <= q_idx)
        o_ref[kj] = jnp.where(keep, 0.0, NEG_BIAS).astype(BF16)
        return carry

    lax.fori_loop(0, n_live, emit, 0)

    def emit_dead(kj, carry):
        o_ref[kj] = jnp.full((tk, tq), NEG_BIAS, BF16)
        return carry

    lax.fori_loop(n_live, nk, emit_dead, 0)


def indexer_bias(q_idx_t, w_t, k_a, k_b, batch, seq, n_heads, k_sel, tq):
    nq = seq // tq
    nk = nq
    w_rows = -(-n_heads // SUBLANES) * SUBLANES
    kspec = pl.BlockSpec((seq, LANES), lambda b, i: (b, 0))
    return pl.pallas_call(
        functools.partial(_indexer_kernel, n_heads=n_heads, k_sel=k_sel, seq=seq),
        grid=(batch, nq),
        in_specs=[pl.BlockSpec((n_heads * IDX_DIM, tq), lambda b, i: (0, b * nq + i)),
                  pl.BlockSpec((w_rows, tq), lambda b, i: (0, b * nq + i)),
                  kspec, kspec],
        out_specs=pl.BlockSpec((None, nk, tq, tq), lambda b, i: (b * nq + i, 0, 0, 0)),
        out_shape=jax.ShapeDtypeStruct((batch * nq, nk, tq, tq), BF16),
        scratch_shapes=[pltpu.VMEM((nk, tq, tq), jnp.int32), pltpu.VMEM((SUBLANES, tq), jnp.int32)],
        compiler_params=_params("parallel", "arbitrary"),
        name="indexer_bias",
    )(q_idx_t, w_t, k_a, k_b)


def _attn_kernel(q_ref, k_ref, v_ref, b_ref, o_ref, qcat_ref, sa_ref, sb_ref, m_ref, l_ref, acc_ref):
    _, tk, tq = b_ref.shape
    qi = pl.program_id(2)
    for h in range(Q_GROUP):
        qcat_ref[:, h * tq:(h + 1) * tq] = q_ref[h * HEAD_DIM:(h + 1) * HEAD_DIM, :]
    m_ref[...] = jnp.full(m_ref.shape, NEG_BIAS, F32)
    l_ref[...] = jnp.zeros(l_ref.shape, F32)
    acc_ref[...] = jnp.zeros(acc_ref.shape, F32)

    def scores(kj):
        k = k_ref[pl.ds(pl.multiple_of(kj * tk, tk), tk), :]
        return jnp.dot(k, qcat_ref[...], preferred_element_type=F32)

    def accumulate(kj, s):
        bias = b_ref[kj].astype(F32)
        s = s + jnp.concatenate([bias] * Q_GROUP, axis=1)
        m_prev = m_ref[...]
        m_new = jnp.maximum(m_prev, jnp.max(s, axis=0, keepdims=True))
        alpha = jnp.exp2(m_prev - m_new)
        p = jnp.exp2(s - m_new)
        l_ref[...] = alpha * l_ref[...] + jnp.sum(p, axis=0, keepdims=True)
        acc_ref[...] = alpha * acc_ref[...] + jnp.dot(v_ref[kj], p.astype(BF16), preferred_element_type=F32)
        m_ref[...] = m_new

    n_blocks = qi + 1
    sa_ref[...] = scores(0)

    def pair(i, carry):
        kj = 2 * i
        sb_ref[...] = scores(kj + 1)
        accumulate(kj, sa_ref[...])
        sa_ref[...] = scores(jnp.minimum(kj + 2, qi))
        accumulate(kj + 1, sb_ref[...])
        return carry

    lax.fori_loop(0, n_blocks // 2, pair, 0)

    @pl.when(n_blocks % 2 == 1)
    def _():
        accumulate(qi, sa_ref[...])

    for h in range(Q_GROUP):
        sl = slice(h * tq, (h + 1) * tq)
        o_ref[:, h * HEAD_DIM:(h + 1) * HEAD_DIM] = (acc_ref[:, sl] / l_ref[:, sl]).T.astype(o_ref.dtype)


def masked_attention(q_t, k, v_t, bias, batch, seq, n_kv):
    _, nk, tk, tq = bias.shape
    nq = seq // tq
    n = batch * seq
    gw = Q_GROUP * HEAD_DIM
    v5 = v_t.reshape(batch, nk, n_kv, HEAD_DIM, tk)
    return pl.pallas_call(
        _attn_kernel,
        grid=(batch, n_kv, nq),
        in_specs=[pl.BlockSpec((gw, tq), lambda b, g, i: (g, b * nq + i)),
                  pl.BlockSpec((seq, HEAD_DIM), lambda b, g, i: (b, g)),
                  pl.BlockSpec((None, nk, None, HEAD_DIM, tk), lambda b, g, i: (b, 0, g, 0, 0)),
                  pl.BlockSpec((None, nk, tk, tq), lambda b, g, i: (b * nq + i, 0, 0, 0))],
        out_specs=pl.BlockSpec((tq, gw), lambda b, g, i: (b * nq + i, g)),
        out_shape=jax.ShapeDtypeStruct((n, n_kv * gw), BF16),
        scratch_shapes=[pltpu.VMEM((HEAD_DIM, Q_GROUP * tq), BF16),
                        pltpu.VMEM((tk, Q_GROUP * tq), F32), pltpu.VMEM((tk, Q_GROUP * tq), F32),
                        pltpu.VMEM((1, Q_GROUP * tq), F32), pltpu.VMEM((1, Q_GROUP * tq), F32),
                        pltpu.VMEM((HEAD_DIM, Q_GROUP * tq), F32)],
        compiler_params=_params("parallel", "parallel", "arbitrary"),
        name="masked_attention",
    )(q_t, k, v5, bias)


def _hgrn_kernel(q_ref, f_ref, i_ref, g_ref, lb_ref, on_ref, o_ref, st_ref, intra_ref, mask_ref, *, chunk):
    tb = q_ref.shape[0]
    n_heads = q_ref.shape[1] // HGRN_EXPAND
    c_rows = chunk

    @pl.when(pl.program_id(2) == 0)
    def _():
        st_ref[...] = jnp.zeros(st_ref.shape, F32)
        rr = lax.broadcasted_iota(jnp.int32, mask_ref.shape, 0)
        cc = lax.broadcasted_iota(jnp.int32, mask_ref.shape, 1)
        mask_ref[...] = jnp.where((cc <= rr) & (cc >= rr - rr % c_rows), 1.0, 0.0)

    rowv = lax.broadcasted_iota(jnp.int32, (c_rows, HGRN_EXPAND), 0)
    onorm = on_ref[...]

    def do_chunk(c, carry):
        rows = pl.ds(pl.multiple_of(c * c_rows, c_rows), c_rows)

        heads = []
        for h in range(n_heads):
            sl = slice(h * HGRN_EXPAND, (h + 1) * HGRN_EXPAND)
            qr = q_ref[rows, sl]
            fr = f_ref[rows, sl]
            lb = lb_ref[:, sl]
            q = qr * _sigmoid(qr)
            e = jnp.exp(-jnp.abs(fr))
            r = 1.0 / (1.0 + e)
            sig_pos = jnp.where(fr >= 0, r, e * r)
            sig_neg = jnp.where(fr >= 0, e * r, r)
            log_f = jnp.log(lb + (1.0 - lb) * sig_pos)
            kk = (1.0 - lb) * sig_neg
            b = log_f
            s = 1
            while s < c_rows:
                b = b + jnp.where(rowv >= s, pltpu.roll(b, s, 0), 0.0)
                s *= 2
            b_last = b[c_rows - 1:c_rows, :]
            heads.append(dict(sl=sl, q=q, kk=kk, b=b, b_last=b_last,
                              qt=(q * jnp.exp(b)).astype(BF16), v=i_ref[rows, sl].astype(BF16)))

        b_min = heads[0]["b_last"]
        for hd in heads[1:]:
            b_min = jnp.minimum(b_min, hd["b_last"])
        safe = jnp.min(b_min) >= -SAFE_DECAY_EXPONENT

        @pl.when(safe)
        def _():
            qt_all = jnp.concatenate([hd["qt"] for hd in heads], axis=0)
            kh_all = jnp.concatenate([(hd["kk"] * jnp.exp(-hd["b"])).astype(BF16) for hd in heads], axis=0)
            v_all = jnp.concatenate([hd["v"] for hd in heads], axis=0)
            a = lax.dot_general(qt_all, kh_all, _NT, preferred_element_type=F32)
            a = jnp.where(mask_ref[...] != 0.0, a, 0.0).astype(BF16)
            o_all = jnp.dot(a, v_all, preferred_element_type=F32)
            for h in range(n_heads):
                intra_ref[h] = o_all[h * c_rows:(h + 1) * c_rows]

        @pl.when(jnp.logical_not(safe))
        def _():
            for h, hd in enumerate(heads):
                q, kk, b = hd["q"], hd["kk"], hd["b"]
                v = i_ref[rows, hd["sl"]]

                def off_step(d, o):
                    kd = pltpu.roll(kk, d, 0)
                    bd = pltpu.roll(b, d, 0)
                    vd = pltpu.roll(v, d, 0)
                    wgt = q * kd * jnp.exp(jnp.minimum(b - bd, 0.0))
                    a = jnp.sum(wgt, axis=1, keepdims=True)
                    return o + jnp.where(rowv >= d, a, 0.0) * vd
                intra_ref[h] = lax.fori_loop(0, c_rows, off_step, jnp.zeros((c_rows, HGRN_EXPAND), F32))

        for h, hd in enumerate(heads):
            st = st_ref[h]
            o = intra_ref[h] + lax.dot_general(hd["qt"], st.astype(BF16), _NT, preferred_element_type=F32)
            k_new = (hd["kk"] * jnp.exp(hd["b_last"] - hd["b"])).astype(BF16)
            st_ref[h] = st * jnp.exp(hd["b_last"]) + lax.dot_general(hd["v"], k_new, _TN,
                                                                      preferred_element_type=F32)
            gr = g_ref[rows, hd["sl"]]
            ms = jnp.mean(o * o, axis=-1, keepdims=True)
            out = o * lax.rsqrt(ms + NORM_EPS) * onorm * (gr * _sigmoid(gr))
            o_ref[rows, hd["sl"]] = out.astype(o_ref.dtype)
        return carry

    lax.fori_loop(0, tb // c_rows, do_chunk, 0)


def hgrn_recurrence(proj, lower_bound, out_norm, batch, seq, d_model, tb=512, heads_per_step=8):
    n = proj.shape[0]
    tb = min(tb, seq)
    nt = seq // tb
    width = min(heads_per_step * HGRN_EXPAND, d_model)
    ng = d_model // width
    chunk = min(HGRN_CHUNK, tb)

    def part(p):
        return pl.BlockSpec((tb, width), lambda b, g, t: (b * nt + t, p * ng + g))

    return pl.pallas_call(
        functools.partial(_hgrn_kernel, chunk=chunk),
        grid=(batch, ng, nt),
        in_specs=[part(0), part(1), part(2), part(3),
                  pl.BlockSpec((1, width), lambda b, g, t: (0, g)),
                  pl.BlockSpec((1, HGRN_EXPAND), lambda b, g, t: (0, 0))],
        out_specs=pl.BlockSpec((tb, width), lambda b, g, t: (b * nt + t, g)),
        out_shape=jax.ShapeDtypeStruct((n, d_model), BF16),
        scratch_shapes=[pltpu.VMEM((width // HGRN_EXPAND, HGRN_EXPAND, HGRN_EXPAND), F32),
                        pltpu.VMEM((width // HGRN_EXPAND, chunk, HGRN_EXPAND), F32),
                        pltpu.VMEM((width // HGRN_EXPAND * chunk, width // HGRN_EXPAND * chunk), F32)],
        compiler_params=_params("parallel", "parallel", "arbitrary"),
        name="hgrn_recurrence",
    )(proj, proj, proj, proj, lower_bound.reshape(1, d_model), out_norm.reshape(1, HGRN_EXPAND))


def _router_kernel(h_ref, rw_ref, rb_ref, o_ref):
    logits = lax.dot_general(rw_ref[...], h_ref[...].astype(BF16), _NT, preferred_element_type=F32)
    s_all = _sigmoid(logits)
    r_all = s_all + rb_ref[...]
    tm = logits.shape[1]

    def rows(x, g):
        return [x[g * EXPERTS_PER_GROUP + j:g * EXPERTS_PER_GROUP + j + 1, :] for j in range(EXPERTS_PER_GROUP)]

    best_score = None
    for g in range(N_GROUPS):
        a, b, c, d = rows(r_all, g)
        hi_ab, lo_ab = jnp.maximum(a, b), jnp.minimum(a, b)
        hi_cd, lo_cd = jnp.maximum(c, d), jnp.minimum(c, d)
        top1 = jnp.maximum(hi_ab, hi_cd)
        top2 = jnp.maximum(jnp.minimum(hi_ab, hi_cd), jnp.maximum(lo_ab, lo_cd))
        score = top1 + top2
        if g == 0:
            best_score, best = score, jnp.zeros((1, tm), F32)
            best_r, best_s = rows(r_all, 0), rows(s_all, 0)
        else:
            upd = score > best_score
            best_score = jnp.where(upd, score, best_score)
            best = jnp.where(upd, float(g), best)
            best_r = [jnp.where(upd, n, o) for n, o in zip(rows(r_all, g), best_r)]
            best_s = [jnp.where(upd, n, o) for n, o in zip(rows(s_all, g), best_s)]

    picked = []
    for j in range(EXPERTS_PER_GROUP):
        rank = jnp.zeros((1, tm), F32)
        for i in range(EXPERTS_PER_GROUP):
            if i == j:
                continue
            ahead = (best_r[i] > best_r[j]) | ((best_r[i] == best_r[j]) & (i < j))
            rank = rank + jnp.where(ahead, 1.0, 0.0)
        picked.append(jnp.where(rank < 2.0, best_s[j], 0.0))
    total = picked[0] + picked[1] + picked[2] + picked[3]
    out_rows = [p / total for p in picked] + [best] + [jnp.zeros((1, tm), F32)] * 3
    o_ref[...] = jnp.concatenate(out_rows, axis=0)


def route(h, router_w_t, router_b, tm=512):
    n, d = h.shape
    tm = _pick_tile(n, tm)
    return pl.pallas_call(
        _router_kernel,
        grid=(n // tm,),
        in_specs=[pl.BlockSpec((tm, d), lambda i: (i, 0)),
                  pl.BlockSpec((N_EXPERTS, d), lambda i: (0, 0)),
                  pl.BlockSpec((N_EXPERTS, 1), lambda i: (0, 0))],
        out_specs=pl.BlockSpec((8, tm), lambda i: (0, i)),
        out_shape=jax.ShapeDtypeStruct((8, n), F32),
        compiler_params=_params("parallel"),
        name="route",
    )(h, router_w_t, router_b.reshape(N_EXPERTS, 1))


def _moe_kernel(grp_ref, cnt_ref, tok_ref, x_hbm, gate_ref, wg_ref, wu_ref, wd_ref, lng_ref, lnb_ref,
                out_hbm, xg_ref, xb_ref, acc_ref, sem_in, sem_out, *, n_tokens, alpha):
    del grp_ref
    _, tm, d = xg_ref.shape
    nb = pl.num_programs(0)
    blk = pl.program_id(0)
    e = pl.program_id(1)
    n_valid = cnt_ref[blk]
    active = n_valid > 0
    slot = blk % 2
    other = 1 - slot
    next_active = (blk + 1 < nb) & (cnt_ref[jnp.minimum(blk + 1, nb - 1)] > 0)

    def start_gather(b, s):
        def issue(r, carry):
            src = jnp.minimum(tok_ref[b * tm + r], n_tokens - 1)
            pltpu.make_async_copy(x_hbm.at[pl.ds(src, 1)], xg_ref.at[s, pl.ds(r, 1)], sem_in.at[s]).start()
            return carry
        lax.fori_loop(0, tm, issue, 0)

    def wait_gather(s):
        pltpu.make_async_copy(x_hbm.at[pl.ds(0, tm)], xg_ref.at[s], sem_in.at[s]).wait()

    def start_scatter(b, s, rows):
        def put(r, carry):
            tok = tok_ref[b * tm + r]
            pltpu.make_async_copy(xg_ref.at[s, pl.ds(r, 1)], out_hbm.at[pl.ds(tok, 1)], sem_out.at[s]).start()
            return carry
        lax.fori_loop(0, rows, put, 0)

    def wait_scatter(s, rows):
        def drain(r, carry):
            pltpu.make_async_copy(xg_ref.at[s, pl.ds(0, 1)], out_hbm.at[pl.ds(0, 1)], sem_out.at[s]).wait()
            return carry
        lax.fori_loop(0, rows, drain, 0)

    @pl.when(active & (e == 0))
    def _():
        @pl.when(blk == 0)
        def _():
            start_gather(0, 0)
        wait_gather(slot)
        xb_ref[...] = xg_ref[slot].astype(BF16)
        acc_ref[...] = jnp.zeros(acc_ref.shape, F32)

    @pl.when(active & (e == 1))
    def _():
        @pl.when(blk > 0)
        def _():
            wait_scatter(other, cnt_ref[jnp.maximum(blk - 1, 0)])

        @pl.when(next_active)
        def _():
            start_gather(blk + 1, other)

    @pl.when(active)
    def _():
        x = xb_ref[...]
        hg = jnp.dot(x, wg_ref[...], preferred_element_type=F32)
        hu = jnp.dot(x, wu_ref[...], preferred_element_type=F32)
        hm = (hg * _sigmoid(hg) * hu).astype(BF16)
        gate = gate_ref[...]
        cols = min(MOE_COL_CHUNK, d)
        for c in range(d // cols):
            sl = slice(c * cols, (c + 1) * cols)
            acc_ref[:, sl] += jnp.dot(hm, wd_ref[:, sl], preferred_element_type=F32) * gate

    @pl.when(active & (e == EXPERTS_PER_GROUP - 1))
    def _():
        xg_ref[slot] = _layer_norm_rows(alpha * xg_ref[slot] + acc_ref[...], lng_ref[...], lnb_ref[...])
        start_scatter(blk, slot, n_valid)

        @pl.when(jnp.logical_not(next_active))
        def _():
            wait_scatter(slot, n_valid)


def moe_ffn_layer_norm(h, routing, wg, wu, wd, layer, ln_g, ln_b, alpha, tm=512):
    n, d = h.shape
    tm = min(tm, n)
    gates = routing[:EXPERTS_PER_GROUP]
    grp = routing[EXPERTS_PER_GROUP].astype(jnp.int32)

    onehot = (grp[:, None] == jnp.arange(N_GROUPS, dtype=jnp.int32)[None, :]).astype(jnp.int32)
    csum = jnp.cumsum(onehot, axis=0)
    counts = csum[-1]
    rank = jnp.sum(csum * onehot, axis=1) - 1
    padded = (counts + tm - 1) // tm * tm
    pends = jnp.cumsum(padded)
    dest = jnp.sum(onehot * (pends - padded)[None, :], axis=1) + rank
    p_rows = n + N_GROUPS * tm
    nb = p_rows // tm
    payload = jnp.concatenate([gates.T, jnp.arange(n, dtype=F32)[:, None]], axis=1)
    empty = jnp.zeros((p_rows, EXPERTS_PER_GROUP + 1), F32).at[:, EXPERTS_PER_GROUP].set(float(n))
    placed = empty.at[dest].set(payload)
    row_tok = placed[:, EXPERTS_PER_GROUP].astype(jnp.int32)
    gate_sorted = placed[:, :EXPERTS_PER_GROUP].T.reshape(EXPERTS_PER_GROUP, p_rows, 1)
    blk_start = jnp.arange(nb, dtype=jnp.int32) * tm
    blk_grp = jnp.minimum(jnp.searchsorted(pends, blk_start, side='right'), N_GROUPS - 1).astype(jnp.int32)
    grp_end = (pends - padded + counts)[blk_grp]
    blk_cnt = jnp.where(blk_start < pends[-1], jnp.clip(grp_end - blk_start, 0, tm), 0).astype(jnp.int32)

    d_exp = wg.shape[-1]
    last = EXPERTS_PER_GROUP - 1

    def w_idx(blk, e, grp_ref, cnt_ref, tok_ref):
        act = jnp.minimum(cnt_ref[blk], 1)
        return layer * N_EXPERTS + grp_ref[blk] * EXPERTS_PER_GROUP + e * act + last * (1 - act)

    grid_spec = pltpu.PrefetchScalarGridSpec(
        num_scalar_prefetch=3,
        grid=(nb, EXPERTS_PER_GROUP),
        in_specs=[pl.BlockSpec(memory_space=pl.ANY),
                  pl.BlockSpec((None, tm, 1), lambda blk, e, *_: (e, blk, 0)),
                  pl.BlockSpec((None, d, d_exp), lambda blk, e, *s: (w_idx(blk, e, *s), 0, 0)),
                  pl.BlockSpec((None, d, d_exp), lambda blk, e, *s: (w_idx(blk, e, *s), 0, 0)),
                  pl.BlockSpec((None, d_exp, d), lambda blk, e, *s: (w_idx(blk, e, *s), 0, 0)),
                  pl.BlockSpec((1, d), lambda blk, e, *_: (0, 0)),
                  pl.BlockSpec((1, d), lambda blk, e, *_: (0, 0))],
        out_specs=pl.BlockSpec(memory_space=pl.ANY),
        scratch_shapes=[pltpu.VMEM((2, tm, d), F32), pltpu.VMEM((tm, d), BF16), pltpu.VMEM((tm, d), F32),
                        pltpu.SemaphoreType.DMA((2,)), pltpu.SemaphoreType.DMA((2,))],
    )
    return pl.pallas_call(
        functools.partial(_moe_kernel, n_tokens=n, alpha=alpha),
        grid_spec=grid_spec,
        out_shape=jax.ShapeDtypeStruct((n, d), F32),
        compiler_params=_params("arbitrary", "arbitrary"),
        name="moe_ffn_layer_norm",
    )(blk_grp, blk_cnt, row_tok, h, gate_sorted, wg, wu, wd, ln_g.reshape(1, d), ln_b.reshape(1, d))


def kernel(x, positions, attn_w_in, attn_q_norm, attn_w_uq, attn_w_uq_idx, attn_kidx_norm_g, attn_kidx_norm_b, attn_w_o, hgrn_w_in, hgrn_lower_bounds, hgrn_out_norm, hgrn_w_o, ln_mix_g, ln_mix_b, ln_ffn_g, ln_ffn_b, router_w, router_b, moe_w_gate, moe_w_up, moe_w_down):
    batch, seq, d = x.shape
    n = batch * seq
    depth = ln_mix_g.shape[0]
    alpha = (2 * depth) ** 0.25
    q_rank = attn_w_uq.shape[1]
    n_q = attn_w_uq.shape[2] // HEAD_DIM
    n_kv = n_q // Q_GROUP
    kv_dim = n_kv * HEAD_DIM
    n_idx = attn_w_uq_idx.shape[2] // IDX_DIM
    k_sel = min(INDEX_TOPK, seq // 4)
    att_in = attn_w_in.shape[2]
    att_pad = q_rank + 2 * kv_dim + LANES - att_in
    tk = min(ATTN_BLOCK, seq)

    w_in_a = jnp.pad(attn_w_in, ((0, 0), (0, 0), (0, att_pad))).astype(BF16)
    w_uq_t = attn_w_uq.transpose(0, 2, 1).astype(BF16)
    w_uq_idx_t = attn_w_uq_idx.transpose(0, 2, 1).astype(BF16)
    w_o_a = attn_w_o.astype(BF16)
    w_in_h = hgrn_w_in.astype(BF16)
    w_o_h = hgrn_w_o.astype(BF16)
    d_exp = moe_w_gate.shape[-1]
    wg = moe_w_gate.astype(BF16).reshape(depth * N_EXPERTS, d, d_exp)
    wu = moe_w_up.astype(BF16).reshape(depth * N_EXPERTS, d, d_exp)
    wd = moe_w_down.astype(BF16).reshape(depth * N_EXPERTS, d_exp, d)
    router_w_t = router_w.T.astype(BF16)
    lb_soft = jax.nn.softmax(hgrn_lower_bounds.astype(F32), axis=0)
    lower_bounds = jnp.cumsum(lb_soft, axis=0) - lb_soft[0]

    c128, s128, c64, s64, ct128, st128, ct64, st64 = rope_tables(positions)

    h = x.reshape(n, d)
    for layer in range(depth):
        j = layer // 2
        if layer % 2 == 0:
            proj = matmul(h, w_in_a[j], tn=640)
            q_t = query_projection_t(proj, q_rank, attn_q_norm[j], w_uq_t[j], ct128, st128, HEAD_DIM,
                                     HEAD_DIM ** -0.5 * LOG2E)
            q_idx_t = query_projection_t(proj, q_rank, attn_q_norm[j], w_uq_idx_t[j], ct64, st64, IDX_DIM, 1.0)
            k, v_t, k_a, k_b, w_t = key_value_projection(
                proj, q_rank, kv_dim, attn_kidx_norm_g[j], attn_kidx_norm_b[j], (c128, s128, c64, s64),
                n_idx ** -0.5 * IDX_DIM ** -0.5, tk)
            bias = indexer_bias(q_idx_t, w_t, k_a, k_b, batch, seq, n_idx, k_sel, tk)
            o = masked_attention(q_t, k, v_t, bias, batch, seq, n_kv)
            w_o = w_o_a[j]
        else:
            proj = matmul(h, w_in_h[j])
            o = hgrn_recurrence(proj, lower_bounds[layer], hgrn_out_norm[j], batch, seq, d)
            w_o = w_o_h[j]
        h = matmul_residual_layer_norm(o, w_o, h, ln_mix_g[layer], ln_mix_b[layer], alpha)
        routing = route(h, router_w_t, router_b)
        h = moe_ffn_layer_norm(h, routing, wg, wu, wd, layer, ln_ffn_g[layer], ln_ffn_b[layer], alpha)
    return h.reshape(batch, seq, d)
```

```python
import functools
import math

import jax
import jax.numpy as jnp
from jax import lax
from jax.experimental import pallas as pl
from jax.experimental.pallas import tpu as pltpu

F32 = jnp.float32
BF16 = jnp.bfloat16

HEAD_DIM = 128
Q_GROUP = 4
IDX_DIM = 64
INDEX_TOPK = 256
ROPE_THETA = 10000.0
HGRN_EXPAND = 128
HGRN_CHUNK = 64
ATTN_BLOCK = 256
V_ROWS = HEAD_DIM + 16
N_EXPERTS = 32
N_GROUPS = 8
EXPERTS_PER_GROUP = N_EXPERTS // N_GROUPS
MOE_COL_CHUNK = 1024
NORM_EPS = 1e-5

LANES = 128
SUBLANES = 8
VMEM_LIMIT_BYTES = 56 * 1024 * 1024

NEG_BIAS = -1e30
INT_MIN = -(2 ** 31)
LOG2E = 1.4426950408889634
SAFE_DECAY_EXPONENT = 80.0

_NT = (((1,), (1,)), ((), ()))
_TN = (((0,), (0,)), ((), ()))


def _pick_tile(n, target, quantum=LANES):
    if n <= target:
        return n
    t = (target // quantum) * quantum
    while t > quantum and n % t:
        t -= quantum
    assert n % t == 0, (n, target)
    return t


def _params(*sem):
    return pltpu.CompilerParams(dimension_semantics=sem, vmem_limit_bytes=VMEM_LIMIT_BYTES)


def _sigmoid(x):
    return 1.0 / (1.0 + jnp.exp(-x))


def _mm_kernel(a_ref, b_ref, o_ref, abf_ref):
    @pl.when(pl.program_id(1) == 0)
    def _():
        abf_ref[...] = a_ref[...].astype(BF16)

    o_ref[...] = jnp.dot(abf_ref[...], b_ref[...], preferred_element_type=F32).astype(o_ref.dtype)


def matmul(a, b, out_dtype=F32, tm=512, tn=1024):
    m, k = a.shape
    _, n = b.shape
    tm = _pick_tile(m, tm, 8)
    tn = _pick_tile(n, tn)
    return pl.pallas_call(
        _mm_kernel,
        grid=(m // tm, n // tn),
        in_specs=[pl.BlockSpec((tm, k), lambda i, j: (i, 0)),
                  pl.BlockSpec((k, tn), lambda i, j: (0, j))],
        out_specs=pl.BlockSpec((tm, tn), lambda i, j: (i, j)),
        out_shape=jax.ShapeDtypeStruct((m, n), out_dtype),
        scratch_shapes=[pltpu.VMEM((tm, k), BF16)],
        compiler_params=_params("parallel", "arbitrary"),
        name="matmul",
    )(a, b)


def _layer_norm_rows(z, g, b):
    mu = jnp.mean(z, axis=-1, keepdims=True)
    zc = z - mu
    var = jnp.mean(zc * zc, axis=-1, keepdims=True)
    return zc * lax.rsqrt(var + NORM_EPS) * g + b


def _res_ln_kernel(h_ref, y_ref, g_ref, b_ref, o_ref, *, alpha):
    o_ref[...] = _layer_norm_rows(alpha * h_ref[...] + y_ref[...], g_ref[...], b_ref[...])


def residual_layer_norm(h, y, g, b, alpha, tm=256):
    n, d = h.shape
    tm = _pick_tile(n, tm, 8)
    row = pl.BlockSpec((tm, d), lambda i: (i, 0))
    vec = pl.BlockSpec((1, d), lambda i: (0, 0))
    return pl.pallas_call(
        functools.partial(_res_ln_kernel, alpha=alpha),
        grid=(n // tm,),
        in_specs=[row, row, vec, vec],
        out_specs=row,
        out_shape=jax.ShapeDtypeStruct((n, d), F32),
        compiler_params=_params("parallel"),
        name="residual_layer_norm",
    )(h, y, g.reshape(1, d), b.reshape(1, d))


def _rope_tab_kernel(pc_ref, pr_ref, f128_ref, f64_ref, fc128_ref, fc64_ref,
                     c128_ref, s128_ref, c64_ref, s64_ref, ct128_ref, st128_ref, ct64_ref, st64_ref):
    pos_col = pc_ref[...]
    pos_row = pr_ref[...]
    lane = lax.broadcasted_iota(jnp.int32, (1, LANES), 1)
    for hd, f_ref, c_ref, s_ref in ((HEAD_DIM, f128_ref, c128_ref, s128_ref),
                                    (IDX_DIM, f64_ref, c64_ref, s64_ref)):
        ang = pos_col * f_ref[...]
        sign = jnp.where((lane % hd) < hd // 2, -1.0, 1.0)
        c_ref[...] = jnp.cos(ang)
        s_ref[...] = jnp.sin(ang) * sign
    for fc_ref, c_ref, s_ref in ((fc128_ref, ct128_ref, st128_ref), (fc64_ref, ct64_ref, st64_ref)):
        ang = fc_ref[...] * pos_row
        c_ref[...] = jnp.cos(ang)
        s_ref[...] = jnp.sin(ang)


def rope_tables(positions):
    n = positions.size
    tm = _pick_tile(n, 512)
    pos = positions.reshape(n).astype(F32)

    def inv_freq(hd):
        return ROPE_THETA ** (-2.0 * jnp.arange(hd // 2, dtype=F32) / hd)

    def lanes(hd):
        return jnp.tile(inv_freq(hd), LANES // (hd // 2)).reshape(1, LANES)

    h128, h64 = HEAD_DIM // 2, IDX_DIM // 2
    tab = pl.BlockSpec((tm, LANES), lambda i: (i, 0))
    vec = pl.BlockSpec((1, LANES), lambda i: (0, 0))
    tok = jax.ShapeDtypeStruct((n, LANES), F32)
    return pl.pallas_call(
        _rope_tab_kernel,
        grid=(n // tm,),
        in_specs=[pl.BlockSpec((tm, 1), lambda i: (i, 0)), pl.BlockSpec((1, tm), lambda i: (0, i)), vec, vec,
                  pl.BlockSpec((h128, 1), lambda i: (0, 0)), pl.BlockSpec((h64, 1), lambda i: (0, 0))],
        out_specs=[tab, tab, tab, tab,
                   pl.BlockSpec((h128, tm), lambda i: (0, i)), pl.BlockSpec((h128, tm), lambda i: (0, i)),
                   pl.BlockSpec((h64, tm), lambda i: (0, i)), pl.BlockSpec((h64, tm), lambda i: (0, i))],
        out_shape=[tok, tok, tok, tok,
                   jax.ShapeDtypeStruct((h128, n), F32), jax.ShapeDtypeStruct((h128, n), F32),
                   jax.ShapeDtypeStruct((h64, n), F32), jax.ShapeDtypeStruct((h64, n), F32)],
        compiler_params=_params("parallel"),
        name="rope_tables",
    )(pos.reshape(n, 1), pos.reshape(1, n), lanes(HEAD_DIM), lanes(IDX_DIM),
      inv_freq(HEAD_DIM).reshape(h128, 1), inv_freq(IDX_DIM).reshape(h64, 1))


def _rotate_half(x, hd):
    if hd == LANES:
        return pltpu.roll(x, LANES // 2, 1)
    lane = lax.broadcasted_iota(jnp.int32, x.shape, 1)
    half = hd // 2
    return jnp.where((lane % hd) < half, pltpu.roll(x, LANES - half, 1), pltpu.roll(x, half, 1))


def _qprep_kernel(cq_ref, g_ref, wt_ref, cos_ref, sin_ref, o_ref, xn_ref, *, hd, scale):
    @pl.when(pl.program_id(1) == 0)
    def _():
        x = cq_ref[...]
        ms = jnp.mean(x * x, axis=-1, keepdims=True)
        xn_ref[...] = (x * lax.rsqrt(ms + NORM_EPS) * g_ref[...]).astype(BF16)

    y = lax.dot_general(wt_ref[...], xn_ref[...], _NT, preferred_element_type=F32)
    cos = cos_ref[...]
    sin = sin_ref[...]
    half = hd // 2
    for hh in range(y.shape[0] // hd):
        lo = hh * hd
        x1 = y[lo:lo + half]
        x2 = y[lo + half:lo + hd]
        o_ref[lo:lo + half, :] = ((x1 * cos - x2 * sin) * scale).astype(o_ref.dtype)
        o_ref[lo + half:lo + hd, :] = ((x2 * cos + x1 * sin) * scale).astype(o_ref.dtype)


def query_projection_t(proj, q_rank, q_norm, w_t, cos_t, sin_t, hd, scale, tm=512, tn=1024):
    n = proj.shape[0]
    nout = w_t.shape[0]
    tm = _pick_tile(n, tm)
    tn = _pick_tile(nout, tn)
    tab = pl.BlockSpec((hd // 2, tm), lambda i, j: (0, i))
    return pl.pallas_call(
        functools.partial(_qprep_kernel, hd=hd, scale=scale),
        grid=(n // tm, nout // tn),
        in_specs=[pl.BlockSpec((tm, q_rank), lambda i, j: (i, 0)),
                  pl.BlockSpec((1, q_rank), lambda i, j: (0, 0)),
                  pl.BlockSpec((tn, q_rank), lambda i, j: (j, 0)),
                  tab, tab],
        out_specs=pl.BlockSpec((tn, tm), lambda i, j: (j, i)),
        out_shape=jax.ShapeDtypeStruct((nout, n), BF16),
        scratch_shapes=[pltpu.VMEM((tm, q_rank), BF16)],
        compiler_params=_params("parallel", "arbitrary"),
        name="query_projection",
    )(proj, q_norm.reshape(1, q_rank), w_t, cos_t, sin_t)


def _kprep_kernel(k_ref, v_ref, t_ref, g_ref, b_ref, c128_ref, s128_ref, c64_ref, s64_ref,
                  ko_ref, vt_ref, ka_ref, kb_ref, wt_ref, *, w_scale):
    cos = c128_ref[...]
    sin = s128_ref[...]
    for c in range(k_ref.shape[1] // LANES):
        sl = slice(c * LANES, (c + 1) * LANES)
        x = k_ref[:, sl]
        ko_ref[:, sl] = (x * cos + _rotate_half(x, HEAD_DIM) * sin).astype(BF16)
        vt_ref[c, :HEAD_DIM, :] = v_ref[:, sl].T.astype(BF16)
        vt_ref[c, HEAD_DIM:, :] = jnp.ones((V_ROWS - HEAD_DIM, v_ref.shape[0]), BF16)

    t = t_ref[...]
    lane = lax.broadcasted_iota(jnp.int32, t.shape, 1)
    is_key = lane < IDX_DIM
    mu = jnp.sum(jnp.where(is_key, t, 0.0), axis=-1, keepdims=True) * (1.0 / IDX_DIM)
    tc = jnp.where(is_key, t - mu, 0.0)
    var = jnp.sum(tc * tc, axis=-1, keepdims=True) * (1.0 / IDX_DIM)
    y = tc * lax.rsqrt(var + NORM_EPS) * g_ref[...] + b_ref[...]
    y = y * c64_ref[...] + _rotate_half(y, IDX_DIM) * s64_ref[...]
    ka_ref[...] = y.astype(BF16)
    kb_ref[...] = pltpu.roll(y, IDX_DIM, 1).astype(BF16)
    wt_ref[...] = t.T[IDX_DIM:, :] * w_scale


def key_value_projection(proj, q_rank, kv_dim, kidx_g, kidx_b, tabs, w_scale, tk):
    n = proj.shape[0]
    assert q_rank % kv_dim == 0 and (q_rank + 2 * kv_dim) % LANES == 0
    kb0 = q_rank // kv_dim
    tail_blk = (q_rank + 2 * kv_dim) // LANES
    n_kv = kv_dim // HEAD_DIM
    pad = LANES - IDX_DIM
    g = jnp.pad(kidx_g, (0, pad)).reshape(1, LANES)
    b = jnp.pad(kidx_b, (0, pad)).reshape(1, LANES)
    tab = pl.BlockSpec((tk, LANES), lambda i: (i, 0))
    vec = pl.BlockSpec((1, LANES), lambda i: (0, 0))
    return pl.pallas_call(
        functools.partial(_kprep_kernel, w_scale=w_scale),
        grid=(n // tk,),
        in_specs=[pl.BlockSpec((tk, kv_dim), lambda i: (i, kb0)),
                  pl.BlockSpec((tk, kv_dim), lambda i: (i, kb0 + 1)),
                  pl.BlockSpec((tk, LANES), lambda i: (i, tail_blk)),
                  vec, vec, tab, tab, tab, tab],
        out_specs=[pl.BlockSpec((tk, kv_dim), lambda i: (i, 0)),
                   pl.BlockSpec((None, n_kv, V_ROWS, tk), lambda i: (i, 0, 0, 0)),
                   tab, tab,
                   pl.BlockSpec((LANES - IDX_DIM, tk), lambda i: (0, i))],
        out_shape=[jax.ShapeDtypeStruct((n, kv_dim), BF16),
                   jax.ShapeDtypeStruct((n // tk, n_kv, V_ROWS, tk), BF16),
                   jax.ShapeDtypeStruct((n, LANES), BF16), jax.ShapeDtypeStruct((n, LANES), BF16),
                   jax.ShapeDtypeStruct((LANES - IDX_DIM, n), F32)],
        compiler_params=_params("parallel"),
        name="key_value_projection",
    )(proj, proj, proj, g, b, *tabs)


def _indexer_kernel(q_ref, w_ref, ka_ref, kb_ref, o_ref, key_ref, cut_ref, *, n_heads, k_sel, seq):
    nk, tk, tq = key_ref.shape
    qi = pl.program_id(1)
    n_live = qi + 1
    key_idx0 = lax.broadcasted_iota(jnp.int32, (tk, tq), 0)
    q_idx = lax.broadcasted_iota(jnp.int32, (tk, tq), 1) + qi * tq

    def score_block(kj, carry):
        off = pl.multiple_of(kj * tk, tk)
        ka = ka_ref[pl.ds(off, tk), :]
        kb = kb_ref[pl.ds(off, tk), :]
        acc = jnp.zeros((tk, tq), F32)
        for p in range(n_heads // 2):
            qp = q_ref[p * LANES:(p + 1) * LANES, :]
            la = jnp.dot(ka, qp, preferred_element_type=F32)
            lb = jnp.dot(kb, qp, preferred_element_type=F32)
            acc = acc + jnp.maximum(la, 0.0) * w_ref[2 * p:2 * p + 1, :]
            acc = acc + jnp.maximum(lb, 0.0) * w_ref[2 * p + 1:2 * p + 2, :]
        s = jnp.where(key_idx0 + kj * tk <= q_idx, acc, -jnp.inf)
        bits = pltpu.bitcast(s, jnp.int32)
        key_ref[kj] = bits ^ ((bits >> 31) & 0x7FFFFFFF)
        return carry

    lax.fori_loop(0, n_live, score_block, 0)

    sub = lax.broadcasted_iota(jnp.int32, (SUBLANES, tq), 0)

    def count(pred):
        def body(kj, c):
            for r in range(tk // SUBLANES):
                key = key_ref[kj, r * SUBLANES:(r + 1) * SUBLANES, :]
                idx = sub + (kj * tk + r * SUBLANES)
                c = c + jnp.where(pred(key, idx), 1.0, 0.0)
            return c
        c = lax.fori_loop(0, n_live, body, jnp.zeros((SUBLANES, tq), F32))
        return jnp.broadcast_to(jnp.sum(c, axis=0, keepdims=True), (SUBLANES, tq))

    def bit_step(i, u):
        cand = u | lax.shift_left(jnp.int32(1), 31 - i)
        cand_signed = cand ^ INT_MIN
        cnt = count(lambda key, idx: key >= cand_signed)
        return jnp.where(cnt >= k_sel, cand, u)

    u = lax.fori_loop(0, 32, bit_step, jnp.zeros((SUBLANES, tq), jnp.int32))
    thr = u ^ INT_MIN

    n_gt = count(lambda key, idx: key > thr)
    n_ge = count(lambda key, idx: key >= thr)

    cut_ref[...] = jnp.full((SUBLANES, tq), seq, jnp.int32)

    @pl.when(jnp.max(n_ge) > k_sel)
    def _():
        need = k_sel - n_gt
        n_bits = max(1, (seq - 1).bit_length())

        def idx_step(i, p):
            t = p + lax.shift_left(jnp.int32(1), n_bits - 1 - i)
            cnt = count(lambda key, idx: (key == thr) & (idx < t))
            return jnp.where(cnt < need, t, p)

        cut_ref[...] = lax.fori_loop(0, n_bits, idx_step, jnp.zeros((SUBLANES, tq), jnp.int32))

    thr_b = jnp.broadcast_to(thr[:1], (tk, tq))
    cut_b = jnp.broadcast_to(cut_ref[:1, :], (tk, tq))

    def emit(kj, carry):
        key = key_ref[kj]
        idx = key_idx0 + kj * tk
        keep = (key > thr_b) | ((key == thr_b) & (idx <= cut_b))
        keep = keep & (idx <= q_idx)
        o_ref[kj] = jnp.where(keep, 0.0, NEG_BIAS).astype(BF16)
        return carry

    lax.fori_loop(0, n_live, emit, 0)

    def emit_dead(kj, carry):
        o_ref[kj] = jnp.full((tk, tq), NEG_BIAS, BF16)
        return carry

    lax.fori_loop(n_live, nk, emit_dead, 0)


def indexer_bias(q_idx_t, w_t, k_a, k_b, batch, seq, n_heads, k_sel, tq):
    nq = seq // tq
    nk = nq
    w_rows = -(-n_heads // SUBLANES) * SUBLANES
    kspec = pl.BlockSpec((seq, LANES), lambda b, i: (b, 0))
    return pl.pallas_call(
        functools.partial(_indexer_kernel, n_heads=n_heads, k_sel=k_sel, seq=seq),
        grid=(batch, nq),
        in_specs=[pl.BlockSpec((n_heads * IDX_DIM, tq), lambda b, i: (0, b * nq + i)),
                  pl.BlockSpec((w_rows, tq), lambda b, i: (0, b * nq + i)),
                  kspec, kspec],
        out_specs=pl.BlockSpec((None, nk, tq, tq), lambda b, i: (b * nq + i, 0, 0, 0)),
        out_shape=jax.ShapeDtypeStruct((batch * nq, nk, tq, tq), BF16),
        scratch_shapes=[pltpu.VMEM((nk, tq, tq), jnp.int32), pltpu.VMEM((SUBLANES, tq), jnp.int32)],
        compiler_params=_params("parallel", "arbitrary"),
        name="indexer_bias",
    )(q_idx_t, w_t, k_a, k_b)


def _attn_kernel(q_ref, k_ref, v_ref, b_ref, o_ref, qcat_ref, sa_ref, sb_ref, m_ref, acc_ref):
    _, tk, tq = b_ref.shape
    qi = pl.program_id(2)
    for h in range(Q_GROUP):
        qcat_ref[:, h * tq:(h + 1) * tq] = q_ref[h * HEAD_DIM:(h + 1) * HEAD_DIM, :]
    m_ref[...] = jnp.full(m_ref.shape, NEG_BIAS, F32)
    acc_ref[...] = jnp.zeros(acc_ref.shape, F32)

    def scores(kj):
        k = k_ref[pl.ds(pl.multiple_of(kj * tk, tk), tk), :]
        return jnp.dot(k, qcat_ref[...], preferred_element_type=F32)

    def accumulate(kj, s):
        bias = b_ref[kj].astype(F32)
        s = s + jnp.concatenate([bias] * Q_GROUP, axis=1)
        m_prev = m_ref[...]
        m_new = jnp.maximum(m_prev, jnp.max(s, axis=0, keepdims=True))
        alpha = jnp.exp2(m_prev - m_new)
        p = jnp.exp2(s - m_new).astype(BF16)
        acc_ref[...] = alpha * acc_ref[...] + jnp.dot(v_ref[kj], p, preferred_element_type=F32)
        m_ref[...] = m_new

    n_blocks = qi + 1
    sa_ref[...] = scores(0)

    def pair(i, carry):
        kj = 2 * i
        sb_ref[...] = scores(kj + 1)
        accumulate(kj, sa_ref[...])
        sa_ref[...] = scores(jnp.minimum(kj + 2, qi))
        accumulate(kj + 1, sb_ref[...])
        return carry

    lax.fori_loop(0, n_blocks // 2, pair, 0)

    @pl.when(n_blocks % 2 == 1)
    def _():
        accumulate(qi, sa_ref[...])

    for h in range(Q_GROUP):
        sl = slice(h * tq, (h + 1) * tq)
        out = acc_ref[:HEAD_DIM, sl] / acc_ref[HEAD_DIM:HEAD_DIM + 1, sl]
        o_ref[:, h * HEAD_DIM:(h + 1) * HEAD_DIM] = out.T.astype(o_ref.dtype)


def masked_attention(q_t, k, v_t, bias, batch, seq, n_kv):
    _, nk, tk, tq = bias.shape
    nq = seq // tq
    n = batch * seq
    gw = Q_GROUP * HEAD_DIM
    v5 = v_t.reshape(batch, nk, n_kv, V_ROWS, tk)
    return pl.pallas_call(
        _attn_kernel,
        grid=(batch, n_kv, nq),
        in_specs=[pl.BlockSpec((gw, tq), lambda b, g, i: (g, b * nq + i)),
                  pl.BlockSpec((seq, HEAD_DIM), lambda b, g, i: (b, g)),
                  pl.BlockSpec((None, nk, None, V_ROWS, tk), lambda b, g, i: (b, 0, g, 0, 0)),
                  pl.BlockSpec((None, nk, tk, tq), lambda b, g, i: (b * nq + i, 0, 0, 0))],
        out_specs=pl.BlockSpec((tq, gw), lambda b, g, i: (b * nq + i, g)),
        out_shape=jax.ShapeDtypeStruct((n, n_kv * gw), BF16),
        scratch_shapes=[pltpu.VMEM((HEAD_DIM, Q_GROUP * tq), BF16),
                        pltpu.VMEM((tk, Q_GROUP * tq), F32), pltpu.VMEM((tk, Q_GROUP * tq), F32),
                        pltpu.VMEM((1, Q_GROUP * tq), F32),
                        pltpu.VMEM((V_ROWS, Q_GROUP * tq), F32)],
        compiler_params=_params("parallel", "parallel", "arbitrary"),
        name="masked_attention",
    )(q_t, k, v5, bias)


def _hgrn_kernel(q_ref, f_ref, i_ref, g_ref, lb_ref, on_ref, o_ref, st_ref, intra_ref, mask_ref, *, chunk):
    tb = q_ref.shape[0]
    n_heads = q_ref.shape[1] // HGRN_EXPAND
    c_rows = chunk

    @pl.when(pl.program_id(2) == 0)
    def _():
        st_ref[...] = jnp.zeros(st_ref.shape, F32)
        rr = lax.broadcasted_iota(jnp.int32, mask_ref.shape, 0)
        cc = lax.broadcasted_iota(jnp.int32, mask_ref.shape, 1)
        mask_ref[...] = jnp.where((cc <= rr) & (cc >= rr - rr % c_rows), 1.0, 0.0)

    rowv = lax.broadcasted_iota(jnp.int32, (c_rows, HGRN_EXPAND), 0)
    onorm = on_ref[...]

    def do_chunk(c, carry):
        rows = pl.ds(pl.multiple_of(c * c_rows, c_rows), c_rows)

        heads = []
        for h in range(n_heads):
            sl = slice(h * HGRN_EXPAND, (h + 1) * HGRN_EXPAND)
            qr = q_ref[rows, sl]
            fr = f_ref[rows, sl]
            lb = lb_ref[:, sl]
            q = qr * _sigmoid(qr)
            e = jnp.exp(-jnp.abs(fr))
            r = 1.0 / (1.0 + e)
            sig_pos = jnp.where(fr >= 0, r, e * r)
            sig_neg = jnp.where(fr >= 0, e * r, r)
            log_f = jnp.log(lb + (1.0 - lb) * sig_pos)
            kk = (1.0 - lb) * sig_neg
            b = log_f
            s = 1
            while s < c_rows:
                b = b + jnp.where(rowv >= s, pltpu.roll(b, s, 0), 0.0)
                s *= 2
            b_last = b[c_rows - 1:c_rows, :]
            heads.append(dict(sl=sl, q=q, kk=kk, b=b, b_last=b_last,
                              qt=(q * jnp.exp(b)).astype(BF16), v=i_ref[rows, sl].astype(BF16)))

        b_min = heads[0]["b_last"]
        for hd in heads[1:]:
            b_min = jnp.minimum(b_min, hd["b_last"])
        safe = jnp.min(b_min) >= -SAFE_DECAY_EXPONENT

        @pl.when(safe)
        def _():
            qt_all = jnp.concatenate([hd["qt"] for hd in heads], axis=0)
            kh_all = jnp.concatenate([(hd["kk"] * jnp.exp(-hd["b"])).astype(BF16) for hd in heads], axis=0)
            v_all = jnp.concatenate([hd["v"] for hd in heads], axis=0)
            a = lax.dot_general(qt_all, kh_all, _NT, preferred_element_type=F32)
            a = jnp.where(mask_ref[...] != 0.0, a, 0.0).astype(BF16)
            o_all = jnp.dot(a, v_all, preferred_element_type=F32)
            for h in range(n_heads):
                intra_ref[h] = o_all[h * c_rows:(h + 1) * c_rows]

        @pl.when(jnp.logical_not(safe))
        def _():
            for h, hd in enumerate(heads):
                q, kk, b = hd["q"], hd["kk"], hd["b"]
                v = i_ref[rows, hd["sl"]]

                def off_step(d, o):
                    kd = pltpu.roll(kk, d, 0)
                    bd = pltpu.roll(b, d, 0)
                    vd = pltpu.roll(v, d, 0)
                    wgt = q * kd * jnp.exp(jnp.minimum(b - bd, 0.0))
                    a = jnp.sum(wgt, axis=1, keepdims=True)
                    return o + jnp.where(rowv >= d, a, 0.0) * vd
                intra_ref[h] = lax.fori_loop(0, c_rows, off_step, jnp.zeros((c_rows, HGRN_EXPAND), F32))

        for h, hd in enumerate(heads):
            st = st_ref[h]
            o = intra_ref[h] + lax.dot_general(hd["qt"], st.astype(BF16), _NT, preferred_element_type=F32)
            k_new = (hd["kk"] * jnp.exp(hd["b_last"] - hd["b"])).astype(BF16)
            st_ref[h] = st * jnp.exp(hd["b_last"]) + lax.dot_general(hd["v"], k_new, _TN,
                                                                      preferred_element_type=F32)
            gr = g_ref[rows, hd["sl"]]
            ms = jnp.mean(o * o, axis=-1, keepdims=True)
            out = o * lax.rsqrt(ms + NORM_EPS) * onorm * (gr * _sigmoid(gr))
            o_ref[rows, hd["sl"]] = out.astype(o_ref.dtype)
        return carry

    lax.fori_loop(0, tb // c_rows, do_chunk, 0)


def hgrn_recurrence(proj, lower_bound, out_norm, batch, seq, d_model, tb=512, heads_per_step=8):
    n = proj.shape[0]
    tb = min(tb, seq)
    nt = seq // tb
    width = min(heads_per_step * HGRN_EXPAND, d_model)
    ng = d_model // width
    chunk = min(HGRN_CHUNK, tb)

    def part(p):
        return pl.BlockSpec((tb, width), lambda b, g, t: (b * nt + t, p * ng + g))

    return pl.pallas_call(
        functools.partial(_hgrn_kernel, chunk=chunk),
        grid=(batch, ng, nt),
        in_specs=[part(0), part(1), part(2), part(3),
                  pl.BlockSpec((1, width), lambda b, g, t: (0, g)),
                  pl.BlockSpec((1, HGRN_EXPAND), lambda b, g, t: (0, 0))],
        out_specs=pl.BlockSpec((tb, width), lambda b, g, t: (b * nt + t, g)),
        out_shape=jax.ShapeDtypeStruct((n, d_model), BF16),
        scratch_shapes=[pltpu.VMEM((width // HGRN_EXPAND, HGRN_EXPAND, HGRN_EXPAND), F32),
                        pltpu.VMEM((width // HGRN_EXPAND, chunk, HGRN_EXPAND), F32),
                        pltpu.VMEM((width // HGRN_EXPAND * chunk, width // HGRN_EXPAND * chunk), F32)],
        compiler_params=_params("parallel", "parallel", "arbitrary"),
        name="hgrn_recurrence",
    )(proj, proj, proj, proj, lower_bound.reshape(1, d_model), out_norm.reshape(1, HGRN_EXPAND))


def _router_kernel(h_ref, rw_ref, rb_ref, o_ref):
    logits = lax.dot_general(rw_ref[...], h_ref[...].astype(BF16), _NT, preferred_element_type=F32)
    s_all = _sigmoid(logits)
    r_all = s_all + rb_ref[...]
    tm = logits.shape[1]

    def rows(x, g):
        return [x[g * EXPERTS_PER_GROUP + j:g * EXPERTS_PER_GROUP + j + 1, :] for j in range(EXPERTS_PER_GROUP)]

    best_score = None
    for g in range(N_GROUPS):
        a, b, c, d = rows(r_all, g)
        hi_ab, lo_ab = jnp.maximum(a, b), jnp.minimum(a, b)
        hi_cd, lo_cd = jnp.maximum(c, d), jnp.minimum(c, d)
        top1 = jnp.maximum(hi_ab, hi_cd)
        top2 = jnp.maximum(jnp.minimum(hi_ab, hi_cd), jnp.maximum(lo_ab, lo_cd))
        score = top1 + top2
        if g == 0:
            best_score, best = score, jnp.zeros((1, tm), F32)
            best_r, best_s = rows(r_all, 0), rows(s_all, 0)
        else:
            upd = score > best_score
            best_score = jnp.where(upd, score, best_score)
            best = jnp.where(upd, float(g), best)
            best_r = [jnp.where(upd, n, o) for n, o in zip(rows(r_all, g), best_r)]
            best_s = [jnp.where(upd, n, o) for n, o in zip(rows(s_all, g), best_s)]

    picked = []
    for j in range(EXPERTS_PER_GROUP):
        rank = jnp.zeros((1, tm), F32)
        for i in range(EXPERTS_PER_GROUP):
            if i == j:
                continue
            ahead = (best_r[i] > best_r[j]) | ((best_r[i] == best_r[j]) & (i < j))
            rank = rank + jnp.where(ahead, 1.0, 0.0)
        picked.append(jnp.where(rank < 2.0, best_s[j], 0.0))
    total = picked[0] + picked[1] + picked[2] + picked[3]
    out_rows = [p / total for p in picked] + [best] + [jnp.zeros((1, tm), F32)] * 3
    o_ref[...] = jnp.concatenate(out_rows, axis=0)


def route(h, router_w_t, router_b, tm=512):
    n, d = h.shape
    tm = _pick_tile(n, tm)
    return pl.pallas_call(
        _router_kernel,
        grid=(n // tm,),
        in_specs=[pl.BlockSpec((tm, d), lambda i: (i, 0)),
                  pl.BlockSpec((N_EXPERTS, d), lambda i: (0, 0)),
                  pl.BlockSpec((N_EXPERTS, 1), lambda i: (0, 0))],
        out_specs=pl.BlockSpec((8, tm), lambda i: (0, i)),
        out_shape=jax.ShapeDtypeStruct((8, n), F32),
        compiler_params=_params("parallel"),
        name="route",
    )(h, router_w_t, router_b.reshape(N_EXPERTS, 1))


_PAIR_ORDER = ((0, 1), (0, 2), (1, 2), (1, 3), (0, 3), (2, 3))
_PAIR_OF_CODE = {**{1 << a: next(i for i, p in enumerate(_PAIR_ORDER) if a in p) for a in range(EXPERTS_PER_GROUP)},
                 **{(1 << a) | (1 << b): i for i, (a, b) in enumerate(_PAIR_ORDER)}}


def _moe_kernel(cnt_ref, tok_ref, need_ref, expert_ref, x_hbm, gate_ref, wg_ref, wu_ref, wd_ref, lng_ref,
                lnb_ref, out_hbm, xg_ref, xb_ref, acc_ref, sem_in, sem_out, *, n_tokens, alpha):
    del expert_ref
    _, tm, d = xg_ref.shape
    nb = pl.num_programs(0)
    blk = pl.program_id(0)
    e = pl.program_id(1)
    n_valid = cnt_ref[blk]
    active = n_valid > 0
    slot = blk % 2
    other = 1 - slot
    next_active = (blk + 1 < nb) & (cnt_ref[jnp.minimum(blk + 1, nb - 1)] > 0)

    def start_gather(b, s):
        def issue(r, carry):
            src = jnp.minimum(tok_ref[b * tm + r], n_tokens - 1)
            pltpu.make_async_copy(x_hbm.at[pl.ds(src, 1)], xg_ref.at[s, pl.ds(r, 1)], sem_in.at[s]).start()
            return carry
        lax.fori_loop(0, tm, issue, 0, unroll=8)

    def wait_gather(s):
        pltpu.make_async_copy(x_hbm.at[pl.ds(0, tm)], xg_ref.at[s], sem_in.at[s]).wait()

    def start_scatter(b, s, rows):
        def put(r, carry):
            tok = tok_ref[b * tm + r]
            pltpu.make_async_copy(xg_ref.at[s, pl.ds(r, 1)], out_hbm.at[pl.ds(tok, 1)], sem_out.at[s]).start()
            return carry
        lax.fori_loop(0, rows, put, 0)

    def wait_scatter(s, rows):
        def drain(r, carry):
            pltpu.make_async_copy(xg_ref.at[s, pl.ds(0, 1)], out_hbm.at[pl.ds(0, 1)], sem_out.at[s]).wait()
            return carry
        lax.fori_loop(0, rows, drain, 0)

    @pl.when(active & (e == 0))
    def _():
        @pl.when(blk == 0)
        def _():
            start_gather(0, 0)
        wait_gather(slot)
        xb_ref[...] = xg_ref[slot].astype(BF16)
        acc_ref[...] = jnp.zeros(acc_ref.shape, F32)

    @pl.when(active & (e == 1))
    def _():
        @pl.when(blk > 0)
        def _():
            wait_scatter(other, cnt_ref[jnp.maximum(blk - 1, 0)])

        @pl.when(next_active)
        def _():
            start_gather(blk + 1, other)

    @pl.when(active & (need_ref[blk * EXPERTS_PER_GROUP + e] != 0))
    def _():
        x = xb_ref[...]
        hg = jnp.dot(x, wg_ref[...], preferred_element_type=F32)
        hu = jnp.dot(x, wu_ref[...], preferred_element_type=F32)
        hm = (hg * _sigmoid(hg) * hu).astype(BF16)
        gates = gate_ref[...]
        gate = gates[:, EXPERTS_PER_GROUP - 1:EXPERTS_PER_GROUP]
        for j in range(EXPERTS_PER_GROUP - 2, -1, -1):
            gate = jnp.where(e == j, gates[:, j:j + 1], gate)
        cols = min(MOE_COL_CHUNK, d)
        for c in range(d // cols):
            sl = slice(c * cols, (c + 1) * cols)
            acc_ref[:, sl] += jnp.dot(hm, wd_ref[:, sl], preferred_element_type=F32) * gate

    @pl.when(active & (e == EXPERTS_PER_GROUP - 1))
    def _():
        xg_ref[slot] = _layer_norm_rows(alpha * xg_ref[slot] + acc_ref[...], lng_ref[...], lnb_ref[...])
        start_scatter(blk, slot, n_valid)

        @pl.when(jnp.logical_not(next_active))
        def _():
            wait_scatter(slot, n_valid)


def moe_ffn_layer_norm(h, routing, wg, wu, wd, layer, ln_g, ln_b, alpha, tm=512):
    n, d = h.shape
    tm = min(tm, n)
    gates = routing[:EXPERTS_PER_GROUP]
    grp = routing[EXPERTS_PER_GROUP].astype(jnp.int32)

    code = sum((gates[j] > 0).astype(jnp.int32) << j for j in range(EXPERTS_PER_GROUP))
    pair = sum(jnp.where(code == c, p, 0) for c, p in _PAIR_OF_CODE.items())
    n_pairs = len(_PAIR_ORDER)
    n_cls = N_GROUPS * n_pairs
    cls = grp * n_pairs + pair
    onehot = (cls[:, None] == jnp.arange(n_cls, dtype=jnp.int32)[None, :]).astype(jnp.int32)
    csum = jnp.cumsum(onehot, axis=0)
    cls_counts = csum[-1]
    rank = jnp.sum(csum * onehot, axis=1) - 1
    counts = cls_counts.reshape(N_GROUPS, n_pairs).sum(axis=1)
    padded = (counts + tm - 1) // tm * tm
    pends = jnp.cumsum(padded)
    before = (jnp.cumsum(cls_counts) - cls_counts).reshape(N_GROUPS, n_pairs)
    cls_start = ((pends - padded)[:, None] + before - before[:, :1]).reshape(n_cls)
    dest = jnp.sum(onehot * cls_start[None, :], axis=1) + rank
    p_rows = n + N_GROUPS * tm
    nb = p_rows // tm
    payload = jnp.concatenate([gates.T, jnp.arange(n, dtype=F32)[:, None]], axis=1)
    empty = jnp.zeros((p_rows, EXPERTS_PER_GROUP + 1), F32).at[:, EXPERTS_PER_GROUP].set(float(n))
    placed = empty.at[dest].set(payload)
    row_tok = placed[:, EXPERTS_PER_GROUP].astype(jnp.int32)
    gate_sorted = placed[:, :EXPERTS_PER_GROUP]
    blk_start = jnp.arange(nb, dtype=jnp.int32) * tm
    blk_grp = jnp.minimum(jnp.searchsorted(pends, blk_start, side='right'), N_GROUPS - 1).astype(jnp.int32)
    grp_end = (pends - padded + counts)[blk_grp]
    blk_cnt = jnp.where(blk_start < pends[-1], jnp.clip(grp_end - blk_start, 0, tm), 0).astype(jnp.int32)

    n_steps = nb * EXPERTS_PER_GROUP
    need = (gate_sorted.reshape(nb, tm, EXPERTS_PER_GROUP) > 0).any(axis=1).reshape(n_steps)
    expert = (blk_grp[:, None] * EXPERTS_PER_GROUP
              + jnp.arange(EXPERTS_PER_GROUP, dtype=jnp.int32)[None, :]).reshape(n_steps)
    step = jnp.arange(n_steps, dtype=jnp.int32)
    held = lax.cummax(jnp.where(need, step * N_EXPERTS + expert, -1))
    step_expert = jnp.where(held >= 0, held % N_EXPERTS, expert[0]).astype(jnp.int32)
    step_need = need.astype(jnp.int32)

    d_exp = wg.shape[-1]

    def w_idx(blk, e, cnt_ref, tok_ref, need_ref, expert_ref):
        return layer * N_EXPERTS + expert_ref[blk * EXPERTS_PER_GROUP + e]

    grid_spec = pltpu.PrefetchScalarGridSpec(
        num_scalar_prefetch=4,
        grid=(nb, EXPERTS_PER_GROUP),
        in_specs=[pl.BlockSpec(memory_space=pl.ANY),
                  pl.BlockSpec((tm, EXPERTS_PER_GROUP), lambda blk, e, *_: (blk, 0)),
                  pl.BlockSpec((None, d, d_exp), lambda blk, e, *s: (w_idx(blk, e, *s), 0, 0)),
                  pl.BlockSpec((None, d, d_exp), lambda blk, e, *s: (w_idx(blk, e, *s), 0, 0)),
                  pl.BlockSpec((None, d_exp, d), lambda blk, e, *s: (w_idx(blk, e, *s), 0, 0)),
                  pl.BlockSpec((1, d), lambda blk, e, *_: (0, 0)),
                  pl.BlockSpec((1, d), lambda blk, e, *_: (0, 0))],
        out_specs=pl.BlockSpec(memory_space=pl.ANY),
        scratch_shapes=[pltpu.VMEM((2, tm, d), F32), pltpu.VMEM((tm, d), BF16), pltpu.VMEM((tm, d), F32),
                        pltpu.SemaphoreType.DMA((2,)), pltpu.SemaphoreType.DMA((2,))],
    )
    return pl.pallas_call(
        functools.partial(_moe_kernel, n_tokens=n, alpha=alpha),
        grid_spec=grid_spec,
        out_shape=jax.ShapeDtypeStruct((n, d), F32),
        compiler_params=_params("arbitrary", "arbitrary"),
        name="moe_ffn_layer_norm",
    )(blk_cnt, row_tok, step_need, step_expert, h, gate_sorted, wg, wu, wd, ln_g.reshape(1, d), ln_b.reshape(1, d))


def kernel(x, positions, attn_w_in, attn_q_norm, attn_w_uq, attn_w_uq_idx, attn_kidx_norm_g, attn_kidx_norm_b, attn_w_o, hgrn_w_in, hgrn_lower_bounds, hgrn_out_norm, hgrn_w_o, ln_mix_g, ln_mix_b, ln_ffn_g, ln_ffn_b, router_w, router_b, moe_w_gate, moe_w_up, moe_w_down):
    batch, seq, d = x.shape
    n = batch * seq
    depth = ln_mix_g.shape[0]
    alpha = (2 * depth) ** 0.25
    q_rank = attn_w_uq.shape[1]
    n_q = attn_w_uq.shape[2] // HEAD_DIM
    n_kv = n_q // Q_GROUP
    kv_dim = n_kv * HEAD_DIM
    n_idx = attn_w_uq_idx.shape[2] // IDX_DIM
    k_sel = min(INDEX_TOPK, seq // 4)
    att_in = attn_w_in.shape[2]
    att_pad = q_rank + 2 * kv_dim + LANES - att_in
    tk = min(ATTN_BLOCK, seq)

    d_exp = moe_w_gate.shape[-1]
    wg = moe_w_gate.astype(BF16).reshape(depth * N_EXPERTS, d, d_exp)
    wu = moe_w_up.astype(BF16).reshape(depth * N_EXPERTS, d, d_exp)
    wd = moe_w_down.astype(BF16).reshape(depth * N_EXPERTS, d_exp, d)
    router_w_t = router_w.T.astype(BF16)
    lb_soft = jax.nn.softmax(hgrn_lower_bounds.astype(F32), axis=0)
    lower_bounds = jnp.cumsum(lb_soft, axis=0) - lb_soft[0]

    c128, s128, c64, s64, ct128, st128, ct64, st64 = rope_tables(positions)

    h = x.reshape(n, d)
    for layer in range(depth):
        j = layer // 2
        if layer % 2 == 0:
            w_in = jnp.pad(attn_w_in[j], ((0, 0), (0, att_pad))).astype(BF16)
            proj = matmul(h, w_in, tn=640)
            q_t = query_projection_t(proj, q_rank, attn_q_norm[j], attn_w_uq[j].T.astype(BF16), ct128, st128,
                                     HEAD_DIM, HEAD_DIM ** -0.5 * LOG2E)
            q_idx_t = query_projection_t(proj, q_rank, attn_q_norm[j], attn_w_uq_idx[j].T.astype(BF16), ct64,
                                         st64, IDX_DIM, 1.0)
            k, v_t, k_a, k_b, w_t = key_value_projection(
                proj, q_rank, kv_dim, attn_kidx_norm_g[j], attn_kidx_norm_b[j], (c128, s128, c64, s64),
                n_idx ** -0.5 * IDX_DIM ** -0.5, tk)
            bias = indexer_bias(q_idx_t, w_t, k_a, k_b, batch, seq, n_idx, k_sel, tk)
            o = masked_attention(q_t, k, v_t, bias, batch, seq, n_kv)
            mix = matmul(o, attn_w_o[j].astype(BF16))
        else:
            proj = matmul(h, hgrn_w_in[j].astype(BF16))
            o = hgrn_recurrence(proj, lower_bounds[layer], hgrn_out_norm[j], batch, seq, d)
            mix = matmul(o, hgrn_w_o[j].astype(BF16))
        h = residual_layer_norm(h, mix, ln_mix_g[layer], ln_mix_b[layer], alpha)
        routing = route(h, router_w_t, router_b)
        h = moe_ffn_layer_norm(h, routing, wg, wu, wd, layer, ln_ffn_g[layer], ln_ffn_b[layer], alpha)
    return h.reshape(batch, seq, d)
```

```python
import functools
import math

import jax
import jax.numpy as jnp
from jax import lax
from jax.experimental import pallas as pl
from jax.experimental.pallas import tpu as pltpu

F32 = jnp.float32
BF16 = jnp.bfloat16

HEAD_DIM = 128
Q_GROUP = 4
IDX_DIM = 64
INDEX_TOPK = 256
ROPE_THETA = 10000.0
HGRN_EXPAND = 128
HGRN_CHUNK = 64
ATTN_BLOCK = 512
V_ROWS = HEAD_DIM + 16
N_EXPERTS = 32
N_GROUPS = 8
EXPERTS_PER_GROUP = N_EXPERTS // N_GROUPS
MOE_COL_CHUNK = 1024
NORM_EPS = 1e-5

LANES = 128
SUBLANES = 8
VMEM_LIMIT_BYTES = 56 * 1024 * 1024

NEG_BIAS = -1e30
INT_MIN = -(2 ** 31)
LOG2E = 1.4426950408889634
SAFE_DECAY_EXPONENT = 80.0

_NT = (((1,), (1,)), ((), ()))
_TN = (((0,), (0,)), ((), ()))


def _pick_tile(n, target, quantum=LANES):
    if n <= target:
        return n
    t = (target // quantum) * quantum
    while t > quantum and n % t:
        t -= quantum
    assert n % t == 0, (n, target)
    return t


def _params(*sem):
    return pltpu.CompilerParams(dimension_semantics=sem, vmem_limit_bytes=VMEM_LIMIT_BYTES)


def _sigmoid(x):
    return 1.0 / (1.0 + jnp.exp(-x))


def _mm_kernel(a_ref, b_ref, o_ref, abf_ref):
    @pl.when(pl.program_id(1) == 0)
    def _():
        abf_ref[...] = a_ref[...].astype(BF16)

    o_ref[...] = jnp.dot(abf_ref[...], b_ref[...], preferred_element_type=F32).astype(o_ref.dtype)


def matmul(a, b, out_dtype=F32, tm=512, tn=1024):
    m, k = a.shape
    _, n = b.shape
    tm = _pick_tile(m, tm, 8)
    tn = _pick_tile(n, tn)
    return pl.pallas_call(
        _mm_kernel,
        grid=(m // tm, n // tn),
        in_specs=[pl.BlockSpec((tm, k), lambda i, j: (i, 0)),
                  pl.BlockSpec((k, tn), lambda i, j: (0, j))],
        out_specs=pl.BlockSpec((tm, tn), lambda i, j: (i, j)),
        out_shape=jax.ShapeDtypeStruct((m, n), out_dtype),
        scratch_shapes=[pltpu.VMEM((tm, k), BF16)],
        compiler_params=_params("parallel", "arbitrary"),
        name="matmul",
    )(a, b)


def _layer_norm_rows(z, g, b):
    mu = jnp.mean(z, axis=-1, keepdims=True)
    zc = z - mu
    var = jnp.mean(zc * zc, axis=-1, keepdims=True)
    return zc * lax.rsqrt(var + NORM_EPS) * g + b


def _rope_tab_kernel(pc_ref, pr_ref, f128_ref, f64_ref, fc128_ref, fc64_ref,
                     c128_ref, s128_ref, c64_ref, s64_ref, ct128_ref, st128_ref, ct64_ref, st64_ref):
    pos_col = pc_ref[...]
    pos_row = pr_ref[...]
    lane = lax.broadcasted_iota(jnp.int32, (1, LANES), 1)
    for hd, f_ref, c_ref, s_ref in ((HEAD_DIM, f128_ref, c128_ref, s128_ref),
                                    (IDX_DIM, f64_ref, c64_ref, s64_ref)):
        ang = pos_col * f_ref[...]
        sign = jnp.where((lane % hd) < hd // 2, -1.0, 1.0)
        c_ref[...] = jnp.cos(ang)
        s_ref[...] = jnp.sin(ang) * sign
    for fc_ref, c_ref, s_ref in ((fc128_ref, ct128_ref, st128_ref), (fc64_ref, ct64_ref, st64_ref)):
        ang = fc_ref[...] * pos_row
        c_ref[...] = jnp.cos(ang)
        s_ref[...] = jnp.sin(ang)


def rope_tables(positions):
    n = positions.size
    tm = _pick_tile(n, 512)
    pos = positions.reshape(n).astype(F32)

    def inv_freq(hd):
        return ROPE_THETA ** (-2.0 * jnp.arange(hd // 2, dtype=F32) / hd)

    def lanes(hd):
        return jnp.tile(inv_freq(hd), LANES // (hd // 2)).reshape(1, LANES)

    h128, h64 = HEAD_DIM // 2, IDX_DIM // 2
    tab = pl.BlockSpec((tm, LANES), lambda i: (i, 0))
    vec = pl.BlockSpec((1, LANES), lambda i: (0, 0))
    tok = jax.ShapeDtypeStruct((n, LANES), F32)
    return pl.pallas_call(
        _rope_tab_kernel,
        grid=(n // tm,),
        in_specs=[pl.BlockSpec((tm, 1), lambda i: (i, 0)), pl.BlockSpec((1, tm), lambda i: (0, i)), vec, vec,
                  pl.BlockSpec((h128, 1), lambda i: (0, 0)), pl.BlockSpec((h64, 1), lambda i: (0, 0))],
        out_specs=[tab, tab, tab, tab,
                   pl.BlockSpec((h128, tm), lambda i: (0, i)), pl.BlockSpec((h128, tm), lambda i: (0, i)),
                   pl.BlockSpec((h64, tm), lambda i: (0, i)), pl.BlockSpec((h64, tm), lambda i: (0, i))],
        out_shape=[tok, tok, tok, tok,
                   jax.ShapeDtypeStruct((h128, n), F32), jax.ShapeDtypeStruct((h128, n), F32),
                   jax.ShapeDtypeStruct((h64, n), F32), jax.ShapeDtypeStruct((h64, n), F32)],
        compiler_params=_params("parallel"),
        name="rope_tables",
    )(pos.reshape(n, 1), pos.reshape(1, n), lanes(HEAD_DIM), lanes(IDX_DIM),
      inv_freq(HEAD_DIM).reshape(h128, 1), inv_freq(IDX_DIM).reshape(h64, 1))


def _rotate_half(x, hd):
    if hd == LANES:
        return pltpu.roll(x, LANES // 2, 1)
    lane = lax.broadcasted_iota(jnp.int32, x.shape, 1)
    half = hd // 2
    return jnp.where((lane % hd) < half, pltpu.roll(x, LANES - half, 1), pltpu.roll(x, half, 1))


def _qprep_kernel(cq_ref, g_ref, wt_ref, cos_ref, sin_ref, o_ref, xn_ref, *, hd, scale):
    @pl.when(pl.program_id(1) == 0)
    def _():
        x = cq_ref[...]
        ms = jnp.mean(x * x, axis=-1, keepdims=True)
        xn_ref[...] = (x * lax.rsqrt(ms + NORM_EPS) * g_ref[...]).astype(BF16)

    y = lax.dot_general(wt_ref[...], xn_ref[...], _NT, preferred_element_type=F32)
    cos = cos_ref[...]
    sin = sin_ref[...]
    half = hd // 2
    for hh in range(y.shape[0] // hd):
        lo = hh * hd
        x1 = y[lo:lo + half]
        x2 = y[lo + half:lo + hd]
        o_ref[lo:lo + half, :] = ((x1 * cos - x2 * sin) * scale).astype(o_ref.dtype)
        o_ref[lo + half:lo + hd, :] = ((x2 * cos + x1 * sin) * scale).astype(o_ref.dtype)


def query_projection_t(proj, q_rank, q_norm, w_t, cos_t, sin_t, hd, scale, tm=512, tn=1024):
    n = proj.shape[0]
    nout = w_t.shape[0]
    tm = _pick_tile(n, tm)
    tn = _pick_tile(nout, tn)
    tab = pl.BlockSpec((hd // 2, tm), lambda i, j: (0, i))
    return pl.pallas_call(
        functools.partial(_qprep_kernel, hd=hd, scale=scale),
        grid=(n // tm, nout // tn),
        in_specs=[pl.BlockSpec((tm, q_rank), lambda i, j: (i, 0)),
                  pl.BlockSpec((1, q_rank), lambda i, j: (0, 0)),
                  pl.BlockSpec((tn, q_rank), lambda i, j: (j, 0)),
                  tab, tab],
        out_specs=pl.BlockSpec((tn, tm), lambda i, j: (j, i)),
        out_shape=jax.ShapeDtypeStruct((nout, n), BF16),
        scratch_shapes=[pltpu.VMEM((tm, q_rank), BF16)],
        compiler_params=_params("parallel", "arbitrary"),
        name="query_projection",
    )(proj, q_norm.reshape(1, q_rank), w_t, cos_t, sin_t)


def _kprep_kernel(k_ref, v_ref, t_ref, g_ref, b_ref, c128_ref, s128_ref, c64_ref, s64_ref,
                  ko_ref, vt_ref, ka_ref, kb_ref, wt_ref, *, w_scale):
    cos = c128_ref[...]
    sin = s128_ref[...]
    for c in range(k_ref.shape[1] // LANES):
        sl = slice(c * LANES, (c + 1) * LANES)
        x = k_ref[:, sl]
        ko_ref[:, sl] = (x * cos + _rotate_half(x, HEAD_DIM) * sin).astype(BF16)
        vt_ref[c, :HEAD_DIM, :] = v_ref[:, sl].T.astype(BF16)
        vt_ref[c, HEAD_DIM:, :] = jnp.ones((V_ROWS - HEAD_DIM, v_ref.shape[0]), BF16)

    t = t_ref[...]
    lane = lax.broadcasted_iota(jnp.int32, t.shape, 1)
    is_key = lane < IDX_DIM
    mu = jnp.sum(jnp.where(is_key, t, 0.0), axis=-1, keepdims=True) * (1.0 / IDX_DIM)
    tc = jnp.where(is_key, t - mu, 0.0)
    var = jnp.sum(tc * tc, axis=-1, keepdims=True) * (1.0 / IDX_DIM)
    y = tc * lax.rsqrt(var + NORM_EPS) * g_ref[...] + b_ref[...]
    y = y * c64_ref[...] + _rotate_half(y, IDX_DIM) * s64_ref[...]
    ka_ref[...] = y.astype(BF16)
    kb_ref[...] = pltpu.roll(y, IDX_DIM, 1).astype(BF16)
    wt_ref[...] = t.T[IDX_DIM:, :] * w_scale


def key_value_projection(proj, q_rank, kv_dim, kidx_g, kidx_b, tabs, w_scale, tk):
    n = proj.shape[0]
    assert q_rank % kv_dim == 0 and (q_rank + 2 * kv_dim) % LANES == 0
    kb0 = q_rank // kv_dim
    tail_blk = (q_rank + 2 * kv_dim) // LANES
    n_kv = kv_dim // HEAD_DIM
    pad = LANES - IDX_DIM
    g = jnp.pad(kidx_g, (0, pad)).reshape(1, LANES)
    b = jnp.pad(kidx_b, (0, pad)).reshape(1, LANES)
    tab = pl.BlockSpec((tk, LANES), lambda i: (i, 0))
    vec = pl.BlockSpec((1, LANES), lambda i: (0, 0))
    return pl.pallas_call(
        functools.partial(_kprep_kernel, w_scale=w_scale),
        grid=(n // tk,),
        in_specs=[pl.BlockSpec((tk, kv_dim), lambda i: (i, kb0)),
                  pl.BlockSpec((tk, kv_dim), lambda i: (i, kb0 + 1)),
                  pl.BlockSpec((tk, LANES), lambda i: (i, tail_blk)),
                  vec, vec, tab, tab, tab, tab],
        out_specs=[pl.BlockSpec((tk, kv_dim), lambda i: (i, 0)),
                   pl.BlockSpec((None, n_kv, V_ROWS, tk), lambda i: (i, 0, 0, 0)),
                   tab, tab,
                   pl.BlockSpec((LANES - IDX_DIM, tk), lambda i: (0, i))],
        out_shape=[jax.ShapeDtypeStruct((n, kv_dim), BF16),
                   jax.ShapeDtypeStruct((n // tk, n_kv, V_ROWS, tk), BF16),
                   jax.ShapeDtypeStruct((n, LANES), BF16), jax.ShapeDtypeStruct((n, LANES), BF16),
                   jax.ShapeDtypeStruct((LANES - IDX_DIM, n), F32)],
        compiler_params=_params("parallel"),
        name="key_value_projection",
    )(proj, proj, proj, g, b, *tabs)


def _indexer_kernel(q_ref, w_ref, ka_ref, kb_ref, o_ref, key_ref, cut_ref, *, n_heads, k_sel, seq):
    nk, tk, tq = key_ref.shape
    qi = pl.program_id(1)
    n_live = qi + 1
    key_idx0 = lax.broadcasted_iota(jnp.int32, (tk, tq), 0)
    q_idx = lax.broadcasted_iota(jnp.int32, (tk, tq), 1) + qi * tq

    def score_block(kj, carry):
        off = pl.multiple_of(kj * tk, tk)
        ka = ka_ref[pl.ds(off, tk), :]
        kb = kb_ref[pl.ds(off, tk), :]
        acc = jnp.zeros((tk, tq), F32)
        for p in range(n_heads // 2):
            qp = q_ref[p * LANES:(p + 1) * LANES, :]
            la = jnp.dot(ka, qp, preferred_element_type=F32)
            lb = jnp.dot(kb, qp, preferred_element_type=F32)
            acc = acc + jnp.maximum(la, 0.0) * w_ref[2 * p:2 * p + 1, :]
            acc = acc + jnp.maximum(lb, 0.0) * w_ref[2 * p + 1:2 * p + 2, :]
        s = jnp.where(key_idx0 + kj * tk <= q_idx, acc, -jnp.inf)
        bits = pltpu.bitcast(s, jnp.int32)
        key_ref[kj] = bits ^ ((bits >> 31) & 0x7FFFFFFF)
        return carry

    lax.fori_loop(0, n_live, score_block, 0)

    sub = lax.broadcasted_iota(jnp.int32, (SUBLANES, tq), 0)

    def count(pred):
        def body(kj, c):
            for r in range(tk // SUBLANES):
                key = key_ref[kj, r * SUBLANES:(r + 1) * SUBLANES, :]
                idx = sub + (kj * tk + r * SUBLANES)
                c = c + jnp.where(pred(key, idx), 1.0, 0.0)
            return c
        c = lax.fori_loop(0, n_live, body, jnp.zeros((SUBLANES, tq), F32))
        return jnp.broadcast_to(jnp.sum(c, axis=0, keepdims=True), (SUBLANES, tq))

    def bit_step(i, u):
        cand = u | lax.shift_left(jnp.int32(1), 31 - i)
        cand_signed = cand ^ INT_MIN
        cnt = count(lambda key, idx: key >= cand_signed)
        return jnp.where(cnt >= k_sel, cand, u)

    u = lax.fori_loop(0, 32, bit_step, jnp.zeros((SUBLANES, tq), jnp.int32))
    thr = u ^ INT_MIN

    n_gt = count(lambda key, idx: key > thr)
    n_ge = count(lambda key, idx: key >= thr)

    cut_ref[...] = jnp.full((SUBLANES, tq), seq, jnp.int32)

    @pl.when(jnp.max(n_ge) > k_sel)
    def _():
        need = k_sel - n_gt
        n_bits = max(1, (seq - 1).bit_length())

        def idx_step(i, p):
            t = p + lax.shift_left(jnp.int32(1), n_bits - 1 - i)
            cnt = count(lambda key, idx: (key == thr) & (idx < t))
            return jnp.where(cnt < need, t, p)

        cut_ref[...] = lax.fori_loop(0, n_bits, idx_step, jnp.zeros((SUBLANES, tq), jnp.int32))

    thr_b = jnp.broadcast_to(thr[:1], (tk, tq))
    cut_b = jnp.broadcast_to(cut_ref[:1, :], (tk, tq))

    def emit(kj, carry):
        key = key_ref[kj]
        idx = key_idx0 + kj * tk
        keep = (key > thr_b) | ((key == thr_b) & (idx <= cut_b))
        keep = keep & (idx <= q_idx)
        o_ref[kj] = jnp.where(keep, 0.0, NEG_BIAS).astype(BF16)
        return carry

    lax.fori_loop(0, n_live, emit, 0)

    def emit_dead(kj, carry):
        o_ref[kj] = jnp.full((tk, tq), NEG_BIAS, BF16)
        return carry

    lax.fori_loop(n_live, nk, emit_dead, 0)


def indexer_bias(q_idx_t, w_t, k_a, k_b, batch, seq, n_heads, k_sel, tq):
    nq = seq // tq
    nk = nq
    w_rows = -(-n_heads // SUBLANES) * SUBLANES
    kspec = pl.BlockSpec((seq, LANES), lambda b, i: (b, 0))
    return pl.pallas_call(
        functools.partial(_indexer_kernel, n_heads=n_heads, k_sel=k_sel, seq=seq),
        grid=(batch, nq),
        in_specs=[pl.BlockSpec((n_heads * IDX_DIM, tq), lambda b, i: (0, b * nq + i)),
                  pl.BlockSpec((w_rows, tq), lambda b, i: (0, b * nq + i)),
                  kspec, kspec],
        out_specs=pl.BlockSpec((None, nk, tq, tq), lambda b, i: (b * nq + i, 0, 0, 0)),
        out_shape=jax.ShapeDtypeStruct((batch * nq, nk, tq, tq), BF16),
        scratch_shapes=[pltpu.VMEM((nk, tq, tq), jnp.int32), pltpu.VMEM((SUBLANES, tq), jnp.int32)],
        compiler_params=_params("parallel", "arbitrary"),
        name="indexer_bias",
    )(q_idx_t, w_t, k_a, k_b)


def _attn_kernel(q_ref, k_ref, v_ref, b_ref, o_ref, qcat_ref, sa_ref, sb_ref, m_ref, acc_ref):
    _, tk, tq = b_ref.shape
    qi = pl.program_id(2)
    for h in range(Q_GROUP):
        qcat_ref[:, h * tq:(h + 1) * tq] = q_ref[h * HEAD_DIM:(h + 1) * HEAD_DIM, :]
    m_ref[...] = jnp.full(m_ref.shape, NEG_BIAS, F32)
    acc_ref[...] = jnp.zeros(acc_ref.shape, F32)

    def scores(kj):
        k = k_ref[pl.ds(pl.multiple_of(kj * tk, tk), tk), :]
        return jnp.dot(k, qcat_ref[...], preferred_element_type=F32)

    def accumulate(kj, s):
        bias = b_ref[kj].astype(F32)
        s = s + jnp.concatenate([bias] * Q_GROUP, axis=1)
        m_prev = m_ref[...]
        m_new = jnp.maximum(m_prev, jnp.max(s, axis=0, keepdims=True))
        alpha = jnp.exp2(m_prev - m_new)
        p = jnp.exp2(s - m_new).astype(BF16)
        acc_ref[...] = alpha * acc_ref[...] + jnp.dot(v_ref[kj], p, preferred_element_type=F32)
        m_ref[...] = m_new

    n_blocks = qi + 1
    sa_ref[...] = scores(0)

    def pair(i, carry):
        kj = 2 * i
        sb_ref[...] = scores(kj + 1)
        accumulate(kj, sa_ref[...])
        sa_ref[...] = scores(jnp.minimum(kj + 2, qi))
        accumulate(kj + 1, sb_ref[...])
        return carry

    lax.fori_loop(0, n_blocks // 2, pair, 0)

    @pl.when(n_blocks % 2 == 1)
    def _():
        accumulate(qi, sa_ref[...])

    for h in range(Q_GROUP):
        sl = slice(h * tq, (h + 1) * tq)
        out = acc_ref[:HEAD_DIM, sl] / acc_ref[HEAD_DIM:HEAD_DIM + 1, sl]
        o_ref[:, h * HEAD_DIM:(h + 1) * HEAD_DIM] = out.T.astype(o_ref.dtype)


def masked_attention(q_t, k, v_t, bias, batch, seq, n_kv):
    _, nk, tk, tq = bias.shape
    nq = seq // tq
    n = batch * seq
    gw = Q_GROUP * HEAD_DIM
    v5 = v_t.reshape(batch, nk, n_kv, V_ROWS, tk)
    return pl.pallas_call(
        _attn_kernel,
        grid=(batch, n_kv, nq),
        in_specs=[pl.BlockSpec((gw, tq), lambda b, g, i: (g, b * nq + i)),
                  pl.BlockSpec((seq, HEAD_DIM), lambda b, g, i: (b, g)),
                  pl.BlockSpec((None, nk, None, V_ROWS, tk), lambda b, g, i: (b, 0, g, 0, 0)),
                  pl.BlockSpec((None, nk, tk, tq), lambda b, g, i: (b * nq + i, 0, 0, 0))],
        out_specs=pl.BlockSpec((tq, gw), lambda b, g, i: (b * nq + i, g)),
        out_shape=jax.ShapeDtypeStruct((n, n_kv * gw), BF16),
        scratch_shapes=[pltpu.VMEM((HEAD_DIM, Q_GROUP * tq), BF16),
                        pltpu.VMEM((tk, Q_GROUP * tq), F32), pltpu.VMEM((tk, Q_GROUP * tq), F32),
                        pltpu.VMEM((1, Q_GROUP * tq), F32),
                        pltpu.VMEM((V_ROWS, Q_GROUP * tq), F32)],
        compiler_params=_params("parallel", "parallel", "arbitrary"),
        name="masked_attention",
    )(q_t, k, v5, bias)


def _hgrn_kernel(q_ref, f_ref, i_ref, g_ref, lb_ref, on_ref, o_ref, st_ref, intra_ref, mask_ref, *, chunk):
    tb = q_ref.shape[0]
    n_heads = q_ref.shape[1] // HGRN_EXPAND
    c_rows = chunk

    @pl.when(pl.program_id(2) == 0)
    def _():
        st_ref[...] = jnp.zeros(st_ref.shape, F32)
        rr = lax.broadcasted_iota(jnp.int32, mask_ref.shape, 0)
        cc = lax.broadcasted_iota(jnp.int32, mask_ref.shape, 1)
        mask_ref[...] = jnp.where((cc <= rr) & (cc >= rr - rr % c_rows), 1.0, 0.0)

    rowv = lax.broadcasted_iota(jnp.int32, (c_rows, HGRN_EXPAND), 0)
    onorm = on_ref[...]

    def do_chunk(c, carry):
        rows = pl.ds(pl.multiple_of(c * c_rows, c_rows), c_rows)

        heads = []
        for h in range(n_heads):
            sl = slice(h * HGRN_EXPAND, (h + 1) * HGRN_EXPAND)
            qr = q_ref[rows, sl]
            fr = f_ref[rows, sl]
            lb = lb_ref[:, sl]
            q = qr * _sigmoid(qr)
            e = jnp.exp(-jnp.abs(fr))
            r = 1.0 / (1.0 + e)
            sig_pos = jnp.where(fr >= 0, r, e * r)
            sig_neg = jnp.where(fr >= 0, e * r, r)
            log_f = jnp.log(lb + (1.0 - lb) * sig_pos)
            kk = (1.0 - lb) * sig_neg
            b = log_f
            s = 1
            while s < c_rows:
                b = b + jnp.where(rowv >= s, pltpu.roll(b, s, 0), 0.0)
                s *= 2
            b_last = b[c_rows - 1:c_rows, :]
            heads.append(dict(sl=sl, q=q, kk=kk, b=b, b_last=b_last,
                              qt=(q * jnp.exp(b)).astype(BF16), v=i_ref[rows, sl].astype(BF16)))

        b_min = heads[0]["b_last"]
        for hd in heads[1:]:
            b_min = jnp.minimum(b_min, hd["b_last"])
        safe = jnp.min(b_min) >= -SAFE_DECAY_EXPONENT

        @pl.when(safe)
        def _():
            qt_all = jnp.concatenate([hd["qt"] for hd in heads], axis=0)
            kh_all = jnp.concatenate([(hd["kk"] * jnp.exp(-hd["b"])).astype(BF16) for hd in heads], axis=0)
            v_all = jnp.concatenate([hd["v"] for hd in heads], axis=0)
            a = lax.dot_general(qt_all, kh_all, _NT, preferred_element_type=F32)
            a = jnp.where(mask_ref[...] != 0.0, a, 0.0).astype(BF16)
            o_all = jnp.dot(a, v_all, preferred_element_type=F32)
            for h in range(n_heads):
                intra_ref[h] = o_all[h * c_rows:(h + 1) * c_rows]

        @pl.when(jnp.logical_not(safe))
        def _():
            for h, hd in enumerate(heads):
                q, kk, b = hd["q"], hd["kk"], hd["b"]
                v = i_ref[rows, hd["sl"]]

                def off_step(d, o):
                    kd = pltpu.roll(kk, d, 0)
                    bd = pltpu.roll(b, d, 0)
                    vd = pltpu.roll(v, d, 0)
                    wgt = q * kd * jnp.exp(jnp.minimum(b - bd, 0.0))
                    a = jnp.sum(wgt, axis=1, keepdims=True)
                    return o + jnp.where(rowv >= d, a, 0.0) * vd
                intra_ref[h] = lax.fori_loop(0, c_rows, off_step, jnp.zeros((c_rows, HGRN_EXPAND), F32))

        for h, hd in enumerate(heads):
            st = st_ref[h]
            o = intra_ref[h] + lax.dot_general(hd["qt"], st.astype(BF16), _NT, preferred_element_type=F32)
            k_new = (hd["kk"] * jnp.exp(hd["b_last"] - hd["b"])).astype(BF16)
            st_ref[h] = st * jnp.exp(hd["b_last"]) + lax.dot_general(hd["v"], k_new, _TN,
                                                                      preferred_element_type=F32)
            gr = g_ref[rows, hd["sl"]]
            ms = jnp.mean(o * o, axis=-1, keepdims=True)
            out = o * lax.rsqrt(ms + NORM_EPS) * onorm * (gr * _sigmoid(gr))
            o_ref[rows, hd["sl"]] = out.astype(o_ref.dtype)
        return carry

    lax.fori_loop(0, tb // c_rows, do_chunk, 0)


def hgrn_recurrence(proj, lower_bound, out_norm, batch, seq, d_model, tb=512, heads_per_step=8):
    n = proj.shape[0]
    tb = min(tb, seq)
    nt = seq // tb
    width = min(heads_per_step * HGRN_EXPAND, d_model)
    ng = d_model // width
    chunk = min(HGRN_CHUNK, tb)

    def part(p):
        return pl.BlockSpec((tb, width), lambda b, g, t: (b * nt + t, p * ng + g))

    return pl.pallas_call(
        functools.partial(_hgrn_kernel, chunk=chunk),
        grid=(batch, ng, nt),
        in_specs=[part(0), part(1), part(2), part(3),
                  pl.BlockSpec((1, width), lambda b, g, t: (0, g)),
                  pl.BlockSpec((1, HGRN_EXPAND), lambda b, g, t: (0, 0))],
        out_specs=pl.BlockSpec((tb, width), lambda b, g, t: (b * nt + t, g)),
        out_shape=jax.ShapeDtypeStruct((n, d_model), BF16),
        scratch_shapes=[pltpu.VMEM((width // HGRN_EXPAND, HGRN_EXPAND, HGRN_EXPAND), F32),
                        pltpu.VMEM((width // HGRN_EXPAND, chunk, HGRN_EXPAND), F32),
                        pltpu.VMEM((width // HGRN_EXPAND * chunk, width // HGRN_EXPAND * chunk), F32)],
        compiler_params=_params("parallel", "parallel", "arbitrary"),
        name="hgrn_recurrence",
    )(proj, proj, proj, proj, lower_bound.reshape(1, d_model), out_norm.reshape(1, HGRN_EXPAND))


def _routing(h, router_w_t, router_b):
    logits = lax.dot_general(router_w_t, h.astype(BF16), _NT, preferred_element_type=F32)
    s_all = _sigmoid(logits)
    r_all = s_all + router_b
    tm = logits.shape[1]

    def rows(x, g):
        return [x[g * EXPERTS_PER_GROUP + j:g * EXPERTS_PER_GROUP + j + 1, :] for j in range(EXPERTS_PER_GROUP)]

    best_score = None
    for g in range(N_GROUPS):
        a, b, c, d = rows(r_all, g)
        hi_ab, lo_ab = jnp.maximum(a, b), jnp.minimum(a, b)
        hi_cd, lo_cd = jnp.maximum(c, d), jnp.minimum(c, d)
        top1 = jnp.maximum(hi_ab, hi_cd)
        top2 = jnp.maximum(jnp.minimum(hi_ab, hi_cd), jnp.maximum(lo_ab, lo_cd))
        score = top1 + top2
        if g == 0:
            best_score, best = score, jnp.zeros((1, tm), F32)
            best_r, best_s = rows(r_all, 0), rows(s_all, 0)
        else:
            upd = score > best_score
            best_score = jnp.where(upd, score, best_score)
            best = jnp.where(upd, float(g), best)
            best_r = [jnp.where(upd, n, o) for n, o in zip(rows(r_all, g), best_r)]
            best_s = [jnp.where(upd, n, o) for n, o in zip(rows(s_all, g), best_s)]

    picked = []
    for j in range(EXPERTS_PER_GROUP):
        rank = jnp.zeros((1, tm), F32)
        for i in range(EXPERTS_PER_GROUP):
            if i == j:
                continue
            ahead = (best_r[i] > best_r[j]) | ((best_r[i] == best_r[j]) & (i < j))
            rank = rank + jnp.where(ahead, 1.0, 0.0)
        picked.append(jnp.where(rank < 2.0, best_s[j], 0.0))
    total = picked[0] + picked[1] + picked[2] + picked[3]
    out_rows = [p / total for p in picked] + [best] + [jnp.zeros((1, tm), F32)] * 3
    return jnp.concatenate(out_rows, axis=0)


def _res_ln_route_kernel(h_ref, y_ref, g_ref, b_ref, rw_ref, rb_ref, o_ref, r_ref, *, alpha):
    hn = _layer_norm_rows(alpha * h_ref[...] + y_ref[...], g_ref[...], b_ref[...])
    o_ref[...] = hn
    r_ref[...] = _routing(hn, rw_ref[...], rb_ref[...])


def residual_layer_norm_route(h, y, g, b, alpha, router_w_t, router_b, tm=256):
    n, d = h.shape
    tm = _pick_tile(n, tm)
    row = pl.BlockSpec((tm, d), lambda i: (i, 0))
    vec = pl.BlockSpec((1, d), lambda i: (0, 0))
    return pl.pallas_call(
        functools.partial(_res_ln_route_kernel, alpha=alpha),
        grid=(n // tm,),
        in_specs=[row, row, vec, vec,
                  pl.BlockSpec((N_EXPERTS, d), lambda i: (0, 0)),
                  pl.BlockSpec((N_EXPERTS, 1), lambda i: (0, 0))],
        out_specs=[row, pl.BlockSpec((SUBLANES, tm), lambda i: (0, i))],
        out_shape=[jax.ShapeDtypeStruct((n, d), F32), jax.ShapeDtypeStruct((SUBLANES, n), F32)],
        compiler_params=_params("parallel"),
        name="residual_layer_norm_route",
    )(h, y, g.reshape(1, d), b.reshape(1, d), router_w_t, router_b.reshape(N_EXPERTS, 1))


_PAIR_ORDER = ((0, 1), (0, 2), (1, 2), (1, 3), (0, 3), (2, 3))
_PAIR_OF_CODE = {**{1 << a: next(i for i, p in enumerate(_PAIR_ORDER) if a in p) for a in range(EXPERTS_PER_GROUP)},
                 **{(1 << a) | (1 << b): i for i, (a, b) in enumerate(_PAIR_ORDER)}}


def _moe_kernel(cnt_ref, tok_ref, need_ref, expert_ref, x_hbm, gate_ref, wg_ref, wu_ref, wd_ref, lng_ref,
                lnb_ref, out_hbm, xg_ref, xb_ref, acc_ref, sem_in, sem_out, *, n_tokens, alpha):
    del expert_ref
    _, tm, d = xg_ref.shape
    nb = pl.num_programs(0)
    blk = pl.program_id(0)
    e = pl.program_id(1)
    n_valid = cnt_ref[blk]
    active = n_valid > 0
    slot = blk % 2
    other = 1 - slot
    next_active = (blk + 1 < nb) & (cnt_ref[jnp.minimum(blk + 1, nb - 1)] > 0)

    def start_gather(b, s):
        def issue(r, carry):
            src = jnp.minimum(tok_ref[b * tm + r], n_tokens - 1)
            pltpu.make_async_copy(x_hbm.at[pl.ds(src, 1)], xg_ref.at[s, pl.ds(r, 1)], sem_in.at[s]).start()
            return carry
        lax.fori_loop(0, tm, issue, 0, unroll=8)

    def wait_gather(s):
        pltpu.make_async_copy(x_hbm.at[pl.ds(0, tm)], xg_ref.at[s], sem_in.at[s]).wait()

    def start_scatter(b, s, rows):
        def put(r, carry):
            tok = tok_ref[b * tm + r]
            pltpu.make_async_copy(xg_ref.at[s, pl.ds(r, 1)], out_hbm.at[pl.ds(tok, 1)], sem_out.at[s]).start()
            return carry
        lax.fori_loop(0, rows, put, 0)

    def wait_scatter(s, rows):
        def drain(r, carry):
            pltpu.make_async_copy(xg_ref.at[s, pl.ds(0, 1)], out_hbm.at[pl.ds(0, 1)], sem_out.at[s]).wait()
            return carry
        lax.fori_loop(0, rows, drain, 0)

    @pl.when(active & (e == 0))
    def _():
        @pl.when(blk == 0)
        def _():
            start_gather(0, 0)
        wait_gather(slot)
        xb_ref[...] = xg_ref[slot].astype(BF16)
        acc_ref[...] = jnp.zeros(acc_ref.shape, F32)

    @pl.when(active & (e == 1))
    def _():
        @pl.when(blk > 0)
        def _():
            wait_scatter(other, cnt_ref[jnp.maximum(blk - 1, 0)])

        @pl.when(next_active)
        def _():
            start_gather(blk + 1, other)

    @pl.when(active & (need_ref[blk * EXPERTS_PER_GROUP + e] != 0))
    def _():
        x = xb_ref[...]
        hg = jnp.dot(x, wg_ref[...], preferred_element_type=F32)
        hu = jnp.dot(x, wu_ref[...], preferred_element_type=F32)
        hm = (hg * _sigmoid(hg) * hu).astype(BF16)
        gates = gate_ref[...]
        gate = gates[:, EXPERTS_PER_GROUP - 1:EXPERTS_PER_GROUP]
        for j in range(EXPERTS_PER_GROUP - 2, -1, -1):
            gate = jnp.where(e == j, gates[:, j:j + 1], gate)
        cols = min(MOE_COL_CHUNK, d)
        for c in range(d // cols):
            sl = slice(c * cols, (c + 1) * cols)
            acc_ref[:, sl] += jnp.dot(hm, wd_ref[:, sl], preferred_element_type=F32) * gate

    @pl.when(active & (e == EXPERTS_PER_GROUP - 1))
    def _():
        xg_ref[slot] = _layer_norm_rows(alpha * xg_ref[slot] + acc_ref[...], lng_ref[...], lnb_ref[...])
        start_scatter(blk, slot, n_valid)

        @pl.when(jnp.logical_not(next_active))
        def _():
            wait_scatter(slot, n_valid)


def moe_ffn_layer_norm(h, routing, wg, wu, wd, layer, ln_g, ln_b, alpha, tm=512):
    n, d = h.shape
    tm = min(tm, n)
    gates = routing[:EXPERTS_PER_GROUP]
    grp = routing[EXPERTS_PER_GROUP].astype(jnp.int32)

    code = sum((gates[j] > 0).astype(jnp.int32) << j for j in range(EXPERTS_PER_GROUP))
    pair = sum(jnp.where(code == c, p, 0) for c, p in _PAIR_OF_CODE.items())
    n_pairs = len(_PAIR_ORDER)
    n_cls = N_GROUPS * n_pairs
    cls = grp * n_pairs + pair
    onehot = (cls[:, None] == jnp.arange(n_cls, dtype=jnp.int32)[None, :]).astype(jnp.int32)
    csum = jnp.cumsum(onehot, axis=0)
    cls_counts = csum[-1]
    rank = jnp.sum(csum * onehot, axis=1) - 1
    counts = cls_counts.reshape(N_GROUPS, n_pairs).sum(axis=1)
    padded = (counts + tm - 1) // tm * tm
    pends = jnp.cumsum(padded)
    before = (jnp.cumsum(cls_counts) - cls_counts).reshape(N_GROUPS, n_pairs)
    cls_start = ((pends - padded)[:, None] + before - before[:, :1]).reshape(n_cls)
    dest = jnp.sum(onehot * cls_start[None, :], axis=1) + rank
    p_rows = n + N_GROUPS * tm
    nb = p_rows // tm
    payload = jnp.concatenate([gates.T, jnp.arange(n, dtype=F32)[:, None]], axis=1)
    empty = jnp.zeros((p_rows, EXPERTS_PER_GROUP + 1), F32).at[:, EXPERTS_PER_GROUP].set(float(n))
    placed = empty.at[dest].set(payload)
    row_tok = placed[:, EXPERTS_PER_GROUP].astype(jnp.int32)
    gate_sorted = placed[:, :EXPERTS_PER_GROUP]
    blk_start = jnp.arange(nb, dtype=jnp.int32) * tm
    blk_grp = jnp.minimum(jnp.searchsorted(pends, blk_start, side='right'), N_GROUPS - 1).astype(jnp.int32)
    grp_end = (pends - padded + counts)[blk_grp]
    blk_cnt = jnp.where(blk_start < pends[-1], jnp.clip(grp_end - blk_start, 0, tm), 0).astype(jnp.int32)

    n_steps = nb * EXPERTS_PER_GROUP
    need = (gate_sorted.reshape(nb, tm, EXPERTS_PER_GROUP) > 0).any(axis=1).reshape(n_steps)
    expert = (blk_grp[:, None] * EXPERTS_PER_GROUP
              + jnp.arange(EXPERTS_PER_GROUP, dtype=jnp.int32)[None, :]).reshape(n_steps)
    step = jnp.arange(n_steps, dtype=jnp.int32)
    held = lax.cummax(jnp.where(need, step * N_EXPERTS + expert, -1))
    step_expert = jnp.where(held >= 0, held % N_EXPERTS, expert[0]).astype(jnp.int32)
    step_need = need.astype(jnp.int32)

    d_exp = wg.shape[-1]

    def w_idx(blk, e, cnt_ref, tok_ref, need_ref, expert_ref):
        return layer * N_EXPERTS + expert_ref[blk * EXPERTS_PER_GROUP + e]

    grid_spec = pltpu.PrefetchScalarGridSpec(
        num_scalar_prefetch=4,
        grid=(nb, EXPERTS_PER_GROUP),
        in_specs=[pl.BlockSpec(memory_space=pl.ANY),
                  pl.BlockSpec((tm, EXPERTS_PER_GROUP), lambda blk, e, *_: (blk, 0)),
                  pl.BlockSpec((None, d, d_exp), lambda blk, e, *s: (w_idx(blk, e, *s), 0, 0)),
                  pl.BlockSpec((None, d, d_exp), lambda blk, e, *s: (w_idx(blk, e, *s), 0, 0)),
                  pl.BlockSpec((None, d_exp, d), lambda blk, e, *s: (w_idx(blk, e, *s), 0, 0)),
                  pl.BlockSpec((1, d), lambda blk, e, *_: (0, 0)),
                  pl.BlockSpec((1, d), lambda blk, e, *_: (0, 0))],
        out_specs=pl.BlockSpec(memory_space=pl.ANY),
        scratch_shapes=[pltpu.VMEM((2, tm, d), F32), pltpu.VMEM((tm, d), BF16), pltpu.VMEM((tm, d), F32),
                        pltpu.SemaphoreType.DMA((2,)), pltpu.SemaphoreType.DMA((2,))],
    )
    return pl.pallas_call(
        functools.partial(_moe_kernel, n_tokens=n, alpha=alpha),
        grid_spec=grid_spec,
        out_shape=jax.ShapeDtypeStruct((n, d), F32),
        compiler_params=_params("arbitrary", "arbitrary"),
        name="moe_ffn_layer_norm",
    )(blk_cnt, row_tok, step_need, step_expert, h, gate_sorted, wg, wu, wd, ln_g.reshape(1, d), ln_b.reshape(1, d))


def kernel(x, positions, attn_w_in, attn_q_norm, attn_w_uq, attn_w_uq_idx, attn_kidx_norm_g, attn_kidx_norm_b, attn_w_o, hgrn_w_in, hgrn_lower_bounds, hgrn_out_norm, hgrn_w_o, ln_mix_g, ln_mix_b, ln_ffn_g, ln_ffn_b, router_w, router_b, moe_w_gate, moe_w_up, moe_w_down):
    batch, seq, d = x.shape
    n = batch * seq
    depth = ln_mix_g.shape[0]
    alpha = (2 * depth) ** 0.25
    q_rank = attn_w_uq.shape[1]
    n_q = attn_w_uq.shape[2] // HEAD_DIM
    n_kv = n_q // Q_GROUP
    kv_dim = n_kv * HEAD_DIM
    n_idx = attn_w_uq_idx.shape[2] // IDX_DIM
    k_sel = min(INDEX_TOPK, seq // 4)
    att_in = attn_w_in.shape[2]
    att_pad = q_rank + 2 * kv_dim + LANES - att_in
    tk = min(ATTN_BLOCK, seq)

    d_exp = moe_w_gate.shape[-1]
    wg = moe_w_gate.astype(BF16).reshape(depth * N_EXPERTS, d, d_exp)
    wu = moe_w_up.astype(BF16).reshape(depth * N_EXPERTS, d, d_exp)
    wd = moe_w_down.astype(BF16).reshape(depth * N_EXPERTS, d_exp, d)
    router_w_t = router_w.T.astype(BF16)
    lb_soft = jax.nn.softmax(hgrn_lower_bounds.astype(F32), axis=0)
    lower_bounds = jnp.cumsum(lb_soft, axis=0) - lb_soft[0]

    c128, s128, c64, s64, ct128, st128, ct64, st64 = rope_tables(positions)

    h = x.reshape(n, d)
    for layer in range(depth):
        j = layer // 2
        if layer % 2 == 0:
            w_in = jnp.pad(attn_w_in[j], ((0, 0), (0, att_pad))).astype(BF16)
            proj = matmul(h, w_in, tn=640)
            q_t = query_projection_t(proj, q_rank, attn_q_norm[j], attn_w_uq[j].T.astype(BF16), ct128, st128,
                                     HEAD_DIM, HEAD_DIM ** -0.5 * LOG2E)
            q_idx_t = query_projection_t(proj, q_rank, attn_q_norm[j], attn_w_uq_idx[j].T.astype(BF16), ct64,
                                         st64, IDX_DIM, 1.0)
            k, v_t, k_a, k_b, w_t = key_value_projection(
                proj, q_rank, kv_dim, attn_kidx_norm_g[j], attn_kidx_norm_b[j], (c128, s128, c64, s64),
                n_idx ** -0.5 * IDX_DIM ** -0.5, tk)
            bias = indexer_bias(q_idx_t, w_t, k_a, k_b, batch, seq, n_idx, k_sel, tk)
            o = masked_attention(q_t, k, v_t, bias, batch, seq, n_kv)
            mix = matmul(o, attn_w_o[j].astype(BF16))
        else:
            proj = matmul(h, hgrn_w_in[j].astype(BF16))
            o = hgrn_recurrence(proj, lower_bounds[layer], hgrn_out_norm[j], batch, seq, d)
            mix = matmul(o, hgrn_w_o[j].astype(BF16))
        h, routing = residual_layer_norm_route(h, mix, ln_mix_g[layer], ln_mix_b[layer], alpha,
                                               router_w_t, router_b)
        h = moe_ffn_layer_norm(h, routing, wg, wu, wd, layer, ln_ffn_g[layer], ln_ffn_b[layer], alpha)
    return h.reshape(batch, seq, d)
```

```python
import functools
import math

import jax
import jax.numpy as jnp
from jax import lax
from jax.experimental import pallas as pl
from jax.experimental.pallas import tpu as pltpu

F32 = jnp.float32
BF16 = jnp.bfloat16

HEAD_DIM = 128
Q_GROUP = 4
IDX_DIM = 64
INDEX_TOPK = 256
ROPE_THETA = 10000.0
HGRN_EXPAND = 128
HGRN_CHUNK = 64
ATTN_BLOCK = 512
V_ROWS = HEAD_DIM + 16
N_EXPERTS = 32
N_GROUPS = 8
EXPERTS_PER_GROUP = N_EXPERTS // N_GROUPS
MOE_COL_CHUNK = 1024
NORM_EPS = 1e-5

LANES = 128
SUBLANES = 8
VMEM_LIMIT_BYTES = 56 * 1024 * 1024

NEG_BIAS = -1e30
INT_MIN = -(2 ** 31)
LOG2E = 1.4426950408889634
SAFE_DECAY_EXPONENT = 80.0

_NT = (((1,), (1,)), ((), ()))
_TN = (((0,), (0,)), ((), ()))


def _pick_tile(n, target, quantum=LANES):
    if n <= target:
        return n
    t = (target // quantum) * quantum
    while t > quantum and n % t:
        t -= quantum
    assert n % t == 0, (n, target)
    return t


def _params(*sem):
    return pltpu.CompilerParams(dimension_semantics=sem, vmem_limit_bytes=VMEM_LIMIT_BYTES)


def _sigmoid(x):
    return 1.0 / (1.0 + jnp.exp(-x))


def _mm_kernel(a_ref, b_ref, o_ref, abf_ref):
    @pl.when(pl.program_id(1) == 0)
    def _():
        abf_ref[...] = a_ref[...].astype(BF16)

    o_ref[...] = jnp.dot(abf_ref[...], b_ref[...], preferred_element_type=F32).astype(o_ref.dtype)


def matmul(a, b, layer, out_dtype=F32, tm=512, tn=1024):
    m, k = a.shape
    n = b.shape[2]
    tm = _pick_tile(m, tm, 8)
    tn = _pick_tile(n, tn)
    return pl.pallas_call(
        _mm_kernel,
        grid=(m // tm, n // tn),
        in_specs=[pl.BlockSpec((tm, k), lambda i, j: (i, 0)),
                  pl.BlockSpec((None, k, tn), lambda i, j: (layer, 0, j))],
        out_specs=pl.BlockSpec((tm, tn), lambda i, j: (i, j)),
        out_shape=jax.ShapeDtypeStruct((m, n), out_dtype),
        scratch_shapes=[pltpu.VMEM((tm, k), BF16)],
        compiler_params=_params("parallel", "arbitrary"),
        name="matmul",
    )(a, b)


def _layer_norm_rows(z, g, b):
    mu = jnp.mean(z, axis=-1, keepdims=True)
    zc = z - mu
    var = jnp.mean(zc * zc, axis=-1, keepdims=True)
    return zc * lax.rsqrt(var + NORM_EPS) * g + b


def _rope_tab_kernel(pc_ref, pr_ref, f128_ref, f64_ref, fc128_ref, fc64_ref,
                     c128_ref, s128_ref, c64_ref, s64_ref, ct128_ref, st128_ref, ct64_ref, st64_ref):
    pos_col = pc_ref[...]
    pos_row = pr_ref[...]
    lane = lax.broadcasted_iota(jnp.int32, (1, LANES), 1)
    for hd, f_ref, c_ref, s_ref in ((HEAD_DIM, f128_ref, c128_ref, s128_ref),
                                    (IDX_DIM, f64_ref, c64_ref, s64_ref)):
        ang = pos_col * f_ref[...]
        sign = jnp.where((lane % hd) < hd // 2, -1.0, 1.0)
        c_ref[...] = jnp.cos(ang)
        s_ref[...] = jnp.sin(ang) * sign
    for fc_ref, c_ref, s_ref in ((fc128_ref, ct128_ref, st128_ref), (fc64_ref, ct64_ref, st64_ref)):
        ang = fc_ref[...] * pos_row
        c_ref[...] = jnp.cos(ang)
        s_ref[...] = jnp.sin(ang)


def rope_tables(positions):
    n = positions.size
    tm = _pick_tile(n, 512)
    pos = positions.reshape(n).astype(F32)

    def inv_freq(hd):
        return ROPE_THETA ** (-2.0 * jnp.arange(hd // 2, dtype=F32) / hd)

    def lanes(hd):
        return jnp.tile(inv_freq(hd), LANES // (hd // 2)).reshape(1, LANES)

    h128, h64 = HEAD_DIM // 2, IDX_DIM // 2
    tab = pl.BlockSpec((tm, LANES), lambda i: (i, 0))
    vec = pl.BlockSpec((1, LANES), lambda i: (0, 0))
    tok = jax.ShapeDtypeStruct((n, LANES), F32)
    return pl.pallas_call(
        _rope_tab_kernel,
        grid=(n // tm,),
        in_specs=[pl.BlockSpec((tm, 1), lambda i: (i, 0)), pl.BlockSpec((1, tm), lambda i: (0, i)), vec, vec,
                  pl.BlockSpec((h128, 1), lambda i: (0, 0)), pl.BlockSpec((h64, 1), lambda i: (0, 0))],
        out_specs=[tab, tab, tab, tab,
                   pl.BlockSpec((h128, tm), lambda i: (0, i)), pl.BlockSpec((h128, tm), lambda i: (0, i)),
                   pl.BlockSpec((h64, tm), lambda i: (0, i)), pl.BlockSpec((h64, tm), lambda i: (0, i))],
        out_shape=[tok, tok, tok, tok,
                   jax.ShapeDtypeStruct((h128, n), F32), jax.ShapeDtypeStruct((h128, n), F32),
                   jax.ShapeDtypeStruct((h64, n), F32), jax.ShapeDtypeStruct((h64, n), F32)],
        compiler_params=_params("parallel"),
        name="rope_tables",
    )(pos.reshape(n, 1), pos.reshape(1, n), lanes(HEAD_DIM), lanes(IDX_DIM),
      inv_freq(HEAD_DIM).reshape(h128, 1), inv_freq(IDX_DIM).reshape(h64, 1))


def _rotate_half(x, hd):
    if hd == LANES:
        return pltpu.roll(x, LANES // 2, 1)
    lane = lax.broadcasted_iota(jnp.int32, x.shape, 1)
    half = hd // 2
    return jnp.where((lane % hd) < half, pltpu.roll(x, LANES - half, 1), pltpu.roll(x, half, 1))


def _qprep_kernel(cq_ref, g_ref, wt_ref, cos_ref, sin_ref, o_ref, xn_ref, *, hd, scale):
    @pl.when(pl.program_id(1) == 0)
    def _():
        x = cq_ref[...]
        ms = jnp.mean(x * x, axis=-1, keepdims=True)
        xn_ref[...] = (x * lax.rsqrt(ms + NORM_EPS) * g_ref[...]).astype(BF16)

    y = lax.dot_general(wt_ref[...], xn_ref[...], _NT, preferred_element_type=F32)
    cos = cos_ref[...]
    sin = sin_ref[...]
    half = hd // 2
    for hh in range(y.shape[0] // hd):
        lo = hh * hd
        x1 = y[lo:lo + half]
        x2 = y[lo + half:lo + hd]
        o_ref[lo:lo + half, :] = ((x1 * cos - x2 * sin) * scale).astype(o_ref.dtype)
        o_ref[lo + half:lo + hd, :] = ((x2 * cos + x1 * sin) * scale).astype(o_ref.dtype)


def query_projection_t(proj, q_rank, q_norm, w_t, cos_t, sin_t, hd, scale, tm=512, tn=1024):
    n = proj.shape[0]
    nout = w_t.shape[0]
    tm = _pick_tile(n, tm)
    tn = _pick_tile(nout, tn)
    tab = pl.BlockSpec((hd // 2, tm), lambda i, j: (0, i))
    return pl.pallas_call(
        functools.partial(_qprep_kernel, hd=hd, scale=scale),
        grid=(n // tm, nout // tn),
        in_specs=[pl.BlockSpec((tm, q_rank), lambda i, j: (i, 0)),
                  pl.BlockSpec((1, q_rank), lambda i, j: (0, 0)),
                  pl.BlockSpec((tn, q_rank), lambda i, j: (j, 0)),
                  tab, tab],
        out_specs=pl.BlockSpec((tn, tm), lambda i, j: (j, i)),
        out_shape=jax.ShapeDtypeStruct((nout, n), BF16),
        scratch_shapes=[pltpu.VMEM((tm, q_rank), BF16)],
        compiler_params=_params("parallel", "arbitrary"),
        name="query_projection",
    )(proj, q_norm.reshape(1, q_rank), w_t, cos_t, sin_t)


def _kprep_kernel(k_ref, v_ref, t_ref, g_ref, b_ref, c128_ref, s128_ref, c64_ref, s64_ref,
                  ko_ref, vt_ref, ka_ref, kb_ref, wt_ref, *, w_scale):
    cos = c128_ref[...]
    sin = s128_ref[...]
    for c in range(k_ref.shape[1] // LANES):
        sl = slice(c * LANES, (c + 1) * LANES)
        x = k_ref[:, sl]
        ko_ref[:, sl] = (x * cos + _rotate_half(x, HEAD_DIM) * sin).astype(BF16)
        vt_ref[c, :HEAD_DIM, :] = v_ref[:, sl].T.astype(BF16)
        vt_ref[c, HEAD_DIM:, :] = jnp.ones((V_ROWS - HEAD_DIM, v_ref.shape[0]), BF16)

    t = t_ref[...]
    lane = lax.broadcasted_iota(jnp.int32, t.shape, 1)
    is_key = lane < IDX_DIM
    mu = jnp.sum(jnp.where(is_key, t, 0.0), axis=-1, keepdims=True) * (1.0 / IDX_DIM)
    tc = jnp.where(is_key, t - mu, 0.0)
    var = jnp.sum(tc * tc, axis=-1, keepdims=True) * (1.0 / IDX_DIM)
    y = tc * lax.rsqrt(var + NORM_EPS) * g_ref[...] + b_ref[...]
    y = y * c64_ref[...] + _rotate_half(y, IDX_DIM) * s64_ref[...]
    ka_ref[...] = y.astype(BF16)
    kb_ref[...] = pltpu.roll(y, IDX_DIM, 1).astype(BF16)
    wt_ref[...] = t.T[IDX_DIM:, :] * w_scale


def key_value_projection(proj, q_rank, kv_dim, kidx_g, kidx_b, tabs, w_scale, tk):
    n = proj.shape[0]
    assert q_rank % kv_dim == 0 and (q_rank + 2 * kv_dim) % LANES == 0
    kb0 = q_rank // kv_dim
    tail_blk = (q_rank + 2 * kv_dim) // LANES
    n_kv = kv_dim // HEAD_DIM
    pad = LANES - IDX_DIM
    g = jnp.pad(kidx_g, (0, pad)).reshape(1, LANES)
    b = jnp.pad(kidx_b, (0, pad)).reshape(1, LANES)
    tab = pl.BlockSpec((tk, LANES), lambda i: (i, 0))
    vec = pl.BlockSpec((1, LANES), lambda i: (0, 0))
    return pl.pallas_call(
        functools.partial(_kprep_kernel, w_scale=w_scale),
        grid=(n // tk,),
        in_specs=[pl.BlockSpec((tk, kv_dim), lambda i: (i, kb0)),
                  pl.BlockSpec((tk, kv_dim), lambda i: (i, kb0 + 1)),
                  pl.BlockSpec((tk, LANES), lambda i: (i, tail_blk)),
                  vec, vec, tab, tab, tab, tab],
        out_specs=[pl.BlockSpec((tk, kv_dim), lambda i: (i, 0)),
                   pl.BlockSpec((None, n_kv, V_ROWS, tk), lambda i: (i, 0, 0, 0)),
                   tab, tab,
                   pl.BlockSpec((LANES - IDX_DIM, tk), lambda i: (0, i))],
        out_shape=[jax.ShapeDtypeStruct((n, kv_dim), BF16),
                   jax.ShapeDtypeStruct((n // tk, n_kv, V_ROWS, tk), BF16),
                   jax.ShapeDtypeStruct((n, LANES), BF16), jax.ShapeDtypeStruct((n, LANES), BF16),
                   jax.ShapeDtypeStruct((LANES - IDX_DIM, n), F32)],
        compiler_params=_params("parallel"),
        name="key_value_projection",
    )(proj, proj, proj, g, b, *tabs)


def _indexer_kernel(q_ref, w_ref, ka_ref, kb_ref, o_ref, key_ref, cut_ref, *, n_heads, k_sel, seq):
    nk, tk, tq = key_ref.shape
    qi = pl.program_id(1)
    n_live = qi + 1
    key_idx0 = lax.broadcasted_iota(jnp.int32, (tk, tq), 0)
    q_idx = lax.broadcasted_iota(jnp.int32, (tk, tq), 1) + qi * tq

    def score_block(kj, carry):
        off = pl.multiple_of(kj * tk, tk)
        ka = ka_ref[pl.ds(off, tk), :]
        kb = kb_ref[pl.ds(off, tk), :]
        acc = jnp.zeros((tk, tq), F32)
        for p in range(n_heads // 2):
            qp = q_ref[p * LANES:(p + 1) * LANES, :]
            la = jnp.dot(ka, qp, preferred_element_type=F32)
            lb = jnp.dot(kb, qp, preferred_element_type=F32)
            acc = acc + jnp.maximum(la, 0.0) * w_ref[2 * p:2 * p + 1, :]
            acc = acc + jnp.maximum(lb, 0.0) * w_ref[2 * p + 1:2 * p + 2, :]
        s = jnp.where(key_idx0 + kj * tk <= q_idx, acc, -jnp.inf)
        bits = pltpu.bitcast(s, jnp.int32)
        key_ref[kj] = bits ^ ((bits >> 31) & 0x7FFFFFFF)
        return carry

    lax.fori_loop(0, n_live, score_block, 0)

    sub = lax.broadcasted_iota(jnp.int32, (SUBLANES, tq), 0)

    def count(pred):
        def body(kj, c):
            for r in range(tk // SUBLANES):
                key = key_ref[kj, r * SUBLANES:(r + 1) * SUBLANES, :]
                idx = sub + (kj * tk + r * SUBLANES)
                c = c + jnp.where(pred(key, idx), 1.0, 0.0)
            return c
        c = lax.fori_loop(0, n_live, body, jnp.zeros((SUBLANES, tq), F32))
        return jnp.broadcast_to(jnp.sum(c, axis=0, keepdims=True), (SUBLANES, tq))

    def bit_step(i, carry):
        u, n_u = carry
        cand = u | lax.shift_left(jnp.int32(1), 31 - i)
        cand_signed = cand ^ INT_MIN
        cnt = count(lambda key, idx: key >= cand_signed)
        take = cnt >= k_sel
        return jnp.where(take, cand, u), jnp.where(take, cnt, n_u)

    n_all = jnp.zeros((SUBLANES, tq), F32) + (n_live * tk).astype(F32)
    u, n_ge = lax.fori_loop(0, 32, bit_step, (jnp.zeros((SUBLANES, tq), jnp.int32), n_all))
    thr = u ^ INT_MIN

    cut_ref[...] = jnp.full((SUBLANES, tq), seq, jnp.int32)

    @pl.when(jnp.max(n_ge) > k_sel)
    def _():
        need = k_sel - count(lambda key, idx: key > thr)
        n_bits = max(1, (seq - 1).bit_length())

        def idx_step(i, p):
            t = p + lax.shift_left(jnp.int32(1), n_bits - 1 - i)
            cnt = count(lambda key, idx: (key == thr) & (idx < t))
            return jnp.where(cnt < need, t, p)

        cut_ref[...] = lax.fori_loop(0, n_bits, idx_step, jnp.zeros((SUBLANES, tq), jnp.int32))

    thr_b = jnp.broadcast_to(thr[:1], (tk, tq))
    cut_b = jnp.broadcast_to(cut_ref[:1, :], (tk, tq))

    def emit(kj, carry):
        key = key_ref[kj]
        idx = key_idx0 + kj * tk
        keep = (key > thr_b) | ((key == thr_b) & (idx <= cut_b))
        keep = keep & (idx <= q_idx)
        o_ref[kj] = jnp.where(keep, 0.0, NEG_BIAS).astype(BF16)
        return carry

    lax.fori_loop(0, n_live, emit, 0)

    def emit_dead(kj, carry):
        o_ref[kj] = jnp.full((tk, tq), NEG_BIAS, BF16)
        return carry

    lax.fori_loop(n_live, nk, emit_dead, 0)


def indexer_bias(q_idx_t, w_t, k_a, k_b, batch, seq, n_heads, k_sel, tq):
    nq = seq // tq
    nk = nq
    w_rows = -(-n_heads // SUBLANES) * SUBLANES
    kspec = pl.BlockSpec((seq, LANES), lambda b, i: (b, 0))
    return pl.pallas_call(
        functools.partial(_indexer_kernel, n_heads=n_heads, k_sel=k_sel, seq=seq),
        grid=(batch, nq),
        in_specs=[pl.BlockSpec((n_heads * IDX_DIM, tq), lambda b, i: (0, b * nq + i)),
                  pl.BlockSpec((w_rows, tq), lambda b, i: (0, b * nq + i)),
                  kspec, kspec],
        out_specs=pl.BlockSpec((None, nk, tq, tq), lambda b, i: (b * nq + i, 0, 0, 0)),
        out_shape=jax.ShapeDtypeStruct((batch * nq, nk, tq, tq), BF16),
        scratch_shapes=[pltpu.VMEM((nk, tq, tq), jnp.int32), pltpu.VMEM((SUBLANES, tq), jnp.int32)],
        compiler_params=_params("parallel", "arbitrary"),
        name="indexer_bias",
    )(q_idx_t, w_t, k_a, k_b)


def _attn_kernel(q_ref, k_ref, v_ref, b_ref, o_ref, qcat_ref, sa_ref, sb_ref, m_ref, acc_ref):
    _, tk, tq = b_ref.shape
    qi = pl.program_id(2)
    for h in range(Q_GROUP):
        qcat_ref[:, h * tq:(h + 1) * tq] = q_ref[h * HEAD_DIM:(h + 1) * HEAD_DIM, :]
    m_ref[...] = jnp.full(m_ref.shape, NEG_BIAS, F32)
    acc_ref[...] = jnp.zeros(acc_ref.shape, F32)

    def scores(kj):
        k = k_ref[pl.ds(pl.multiple_of(kj * tk, tk), tk), :]
        return jnp.dot(k, qcat_ref[...], preferred_element_type=F32)

    def accumulate(kj, s):
        bias = b_ref[kj].astype(F32)
        s = s + jnp.concatenate([bias] * Q_GROUP, axis=1)
        m_prev = m_ref[...]
        m_new = jnp.maximum(m_prev, jnp.max(s, axis=0, keepdims=True))
        alpha = jnp.exp2(m_prev - m_new)
        p = jnp.exp2(s - m_new).astype(BF16)
        acc_ref[...] = alpha * acc_ref[...] + jnp.dot(v_ref[kj], p, preferred_element_type=F32)
        m_ref[...] = m_new

    n_blocks = qi + 1
    sa_ref[...] = scores(0)

    def pair(i, carry):
        kj = 2 * i
        sb_ref[...] = scores(kj + 1)
        accumulate(kj, sa_ref[...])
        sa_ref[...] = scores(jnp.minimum(kj + 2, qi))
        accumulate(kj + 1, sb_ref[...])
        return carry

    lax.fori_loop(0, n_blocks // 2, pair, 0)

    @pl.when(n_blocks % 2 == 1)
    def _():
        accumulate(qi, sa_ref[...])

    for h in range(Q_GROUP):
        sl = slice(h * tq, (h + 1) * tq)
        out = acc_ref[:HEAD_DIM, sl] / acc_ref[HEAD_DIM:HEAD_DIM + 1, sl]
        o_ref[:, h * HEAD_DIM:(h + 1) * HEAD_DIM] = out.T.astype(o_ref.dtype)


def masked_attention(q_t, k, v_t, bias, batch, seq, n_kv):
    _, nk, tk, tq = bias.shape
    nq = seq // tq
    n = batch * seq
    gw = Q_GROUP * HEAD_DIM
    v5 = v_t.reshape(batch, nk, n_kv, V_ROWS, tk)
    return pl.pallas_call(
        _attn_kernel,
        grid=(batch, n_kv, nq),
        in_specs=[pl.BlockSpec((gw, tq), lambda b, g, i: (g, b * nq + i)),
                  pl.BlockSpec((seq, HEAD_DIM), lambda b, g, i: (b, g)),
                  pl.BlockSpec((None, nk, None, V_ROWS, tk), lambda b, g, i: (b, 0, g, 0, 0)),
                  pl.BlockSpec((None, nk, tk, tq), lambda b, g, i: (b * nq + i, 0, 0, 0))],
        out_specs=pl.BlockSpec((tq, gw), lambda b, g, i: (b * nq + i, g)),
        out_shape=jax.ShapeDtypeStruct((n, n_kv * gw), BF16),
        scratch_shapes=[pltpu.VMEM((HEAD_DIM, Q_GROUP * tq), BF16),
                        pltpu.VMEM((tk, Q_GROUP * tq), F32), pltpu.VMEM((tk, Q_GROUP * tq), F32),
                        pltpu.VMEM((1, Q_GROUP * tq), F32),
                        pltpu.VMEM((V_ROWS, Q_GROUP * tq), F32)],
        compiler_params=_params("parallel", "parallel", "arbitrary"),
        name="masked_attention",
    )(q_t, k, v5, bias)


def _hgrn_kernel(q_ref, f_ref, i_ref, g_ref, lb_ref, on_ref, o_ref, st_ref, intra_ref, mask_ref, *, chunk):
    tb = q_ref.shape[0]
    n_heads = q_ref.shape[1] // HGRN_EXPAND
    c_rows = chunk

    @pl.when(pl.program_id(2) == 0)
    def _():
        st_ref[...] = jnp.zeros(st_ref.shape, F32)
        rr = lax.broadcasted_iota(jnp.int32, mask_ref.shape, 0)
        cc = lax.broadcasted_iota(jnp.int32, mask_ref.shape, 1)
        mask_ref[...] = jnp.where((cc <= rr) & (cc >= rr - rr % c_rows), 1.0, 0.0)

    rowv = lax.broadcasted_iota(jnp.int32, (c_rows, HGRN_EXPAND), 0)
    onorm = on_ref[...]

    def do_chunk(c, carry):
        rows = pl.ds(pl.multiple_of(c * c_rows, c_rows), c_rows)

        heads = []
        for h in range(n_heads):
            sl = slice(h * HGRN_EXPAND, (h + 1) * HGRN_EXPAND)
            qr = q_ref[rows, sl]
            fr = f_ref[rows, sl]
            lb = lb_ref[:, sl]
            q = qr * _sigmoid(qr)
            e = jnp.exp(-jnp.abs(fr))
            r = 1.0 / (1.0 + e)
            sig_pos = jnp.where(fr >= 0, r, e * r)
            sig_neg = jnp.where(fr >= 0, e * r, r)
            log_f = jnp.log(lb + (1.0 - lb) * sig_pos)
            kk = (1.0 - lb) * sig_neg
            b = log_f
            s = 1
            while s < c_rows:
                b = b + jnp.where(rowv >= s, pltpu.roll(b, s, 0), 0.0)
                s *= 2
            b_last = b[c_rows - 1:c_rows, :]
            heads.append(dict(sl=sl, q=q, kk=kk, b=b, b_last=b_last,
                              qt=(q * jnp.exp(b)).astype(BF16), v=i_ref[rows, sl].astype(BF16)))

        b_min = heads[0]["b_last"]
        for hd in heads[1:]:
            b_min = jnp.minimum(b_min, hd["b_last"])
        safe = jnp.min(b_min) >= -SAFE_DECAY_EXPONENT

        @pl.when(safe)
        def _():
            qt_all = jnp.concatenate([hd["qt"] for hd in heads], axis=0)
            kh_all = jnp.concatenate([(hd["kk"] * jnp.exp(-hd["b"])).astype(BF16) for hd in heads], axis=0)
            v_all = jnp.concatenate([hd["v"] for hd in heads], axis=0)
            a = lax.dot_general(qt_all, kh_all, _NT, preferred_element_type=F32)
            a = jnp.where(mask_ref[...] != 0.0, a, 0.0).astype(BF16)
            o_all = jnp.dot(a, v_all, preferred_element_type=F32)
            for h in range(n_heads):
                intra_ref[h] = o_all[h * c_rows:(h + 1) * c_rows]

        @pl.when(jnp.logical_not(safe))
        def _():
            for h, hd in enumerate(heads):
                q, kk, b = hd["q"], hd["kk"], hd["b"]
                v = i_ref[rows, hd["sl"]]

                def off_step(d, o):
                    kd = pltpu.roll(kk, d, 0)
                    bd = pltpu.roll(b, d, 0)
                    vd = pltpu.roll(v, d, 0)
                    wgt = q * kd * jnp.exp(jnp.minimum(b - bd, 0.0))
                    a = jnp.sum(wgt, axis=1, keepdims=True)
                    return o + jnp.where(rowv >= d, a, 0.0) * vd
                intra_ref[h] = lax.fori_loop(0, c_rows, off_step, jnp.zeros((c_rows, HGRN_EXPAND), F32))

        for h, hd in enumerate(heads):
            st = st_ref[h]
            o = intra_ref[h] + lax.dot_general(hd["qt"], st.astype(BF16), _NT, preferred_element_type=F32)
            k_new = (hd["kk"] * jnp.exp(hd["b_last"] - hd["b"])).astype(BF16)
            st_ref[h] = st * jnp.exp(hd["b_last"]) + lax.dot_general(hd["v"], k_new, _TN,
                                                                      preferred_element_type=F32)
            gr = g_ref[rows, hd["sl"]]
            ms = jnp.mean(o * o, axis=-1, keepdims=True)
            out = o * lax.rsqrt(ms + NORM_EPS) * onorm * (gr * _sigmoid(gr))
            o_ref[rows, hd["sl"]] = out.astype(o_ref.dtype)
        return carry

    lax.fori_loop(0, tb // c_rows, do_chunk, 0)


def hgrn_recurrence(proj, lower_bound, out_norm, batch, seq, d_model, tb=512, heads_per_step=8):
    n = proj.shape[0]
    tb = min(tb, seq)
    nt = seq // tb
    width = min(heads_per_step * HGRN_EXPAND, d_model)
    ng = d_model // width
    chunk = min(HGRN_CHUNK, tb)

    def part(p):
        return pl.BlockSpec((tb, width), lambda b, g, t: (b * nt + t, p * ng + g))

    return pl.pallas_call(
        functools.partial(_hgrn_kernel, chunk=chunk),
        grid=(batch, ng, nt),
        in_specs=[part(0), part(1), part(2), part(3),
                  pl.BlockSpec((1, width), lambda b, g, t: (0, g)),
                  pl.BlockSpec((1, HGRN_EXPAND), lambda b, g, t: (0, 0))],
        out_specs=pl.BlockSpec((tb, width), lambda b, g, t: (b * nt + t, g)),
        out_shape=jax.ShapeDtypeStruct((n, d_model), BF16),
        scratch_shapes=[pltpu.VMEM((width // HGRN_EXPAND, HGRN_EXPAND, HGRN_EXPAND), F32),
                        pltpu.VMEM((width // HGRN_EXPAND, chunk, HGRN_EXPAND), F32),
                        pltpu.VMEM((width // HGRN_EXPAND * chunk, width // HGRN_EXPAND * chunk), F32)],
        compiler_params=_params("parallel", "parallel", "arbitrary"),
        name="hgrn_recurrence",
    )(proj, proj, proj, proj, lower_bound.reshape(1, d_model), out_norm.reshape(1, HGRN_EXPAND))


def _routing(h, router_w_t, router_b):
    logits = lax.dot_general(router_w_t, h.astype(BF16), _NT, preferred_element_type=F32)
    s_all = _sigmoid(logits)
    r_all = s_all + router_b
    tm = logits.shape[1]

    def rows(x, g):
        return [x[g * EXPERTS_PER_GROUP + j:g * EXPERTS_PER_GROUP + j + 1, :] for j in range(EXPERTS_PER_GROUP)]

    best_score = None
    for g in range(N_GROUPS):
        a, b, c, d = rows(r_all, g)
        hi_ab, lo_ab = jnp.maximum(a, b), jnp.minimum(a, b)
        hi_cd, lo_cd = jnp.maximum(c, d), jnp.minimum(c, d)
        top1 = jnp.maximum(hi_ab, hi_cd)
        top2 = jnp.maximum(jnp.minimum(hi_ab, hi_cd), jnp.maximum(lo_ab, lo_cd))
        score = top1 + top2
        if g == 0:
            best_score, best = score, jnp.zeros((1, tm), F32)
            best_r, best_s = rows(r_all, 0), rows(s_all, 0)
        else:
            upd = score > best_score
            best_score = jnp.where(upd, score, best_score)
            best = jnp.where(upd, float(g), best)
            best_r = [jnp.where(upd, n, o) for n, o in zip(rows(r_all, g), best_r)]
            best_s = [jnp.where(upd, n, o) for n, o in zip(rows(s_all, g), best_s)]

    picked = []
    for j in range(EXPERTS_PER_GROUP):
        rank = jnp.zeros((1, tm), F32)
        for i in range(EXPERTS_PER_GROUP):
            if i == j:
                continue
            ahead = (best_r[i] > best_r[j]) | ((best_r[i] == best_r[j]) & (i < j))
            rank = rank + jnp.where(ahead, 1.0, 0.0)
        picked.append(jnp.where(rank < 2.0, best_s[j], 0.0))
    total = picked[0] + picked[1] + picked[2] + picked[3]
    out_rows = [p / total for p in picked] + [best] + [jnp.zeros((1, tm), F32)] * 3
    return jnp.concatenate(out_rows, axis=0)


def _res_ln_route_kernel(h_ref, y_ref, g_ref, b_ref, rw_ref, rb_ref, o_ref, r_ref, *, alpha):
    hn = _layer_norm_rows(alpha * h_ref[...] + y_ref[...], g_ref[...], b_ref[...])
    o_ref[...] = hn
    r_ref[...] = _routing(hn, rw_ref[...], rb_ref[...])


def residual_layer_norm_route(h, y, g, b, alpha, router_w_t, router_b, tm=256):
    n, d = h.shape
    tm = _pick_tile(n, tm)
    row = pl.BlockSpec((tm, d), lambda i: (i, 0))
    vec = pl.BlockSpec((1, d), lambda i: (0, 0))
    return pl.pallas_call(
        functools.partial(_res_ln_route_kernel, alpha=alpha),
        grid=(n // tm,),
        in_specs=[row, row, vec, vec,
                  pl.BlockSpec((N_EXPERTS, d), lambda i: (0, 0)),
                  pl.BlockSpec((N_EXPERTS, 1), lambda i: (0, 0))],
        out_specs=[row, pl.BlockSpec((SUBLANES, tm), lambda i: (0, i))],
        out_shape=[jax.ShapeDtypeStruct((n, d), F32), jax.ShapeDtypeStruct((SUBLANES, n), F32)],
        compiler_params=_params("parallel"),
        name="residual_layer_norm_route",
    )(h, y, g.reshape(1, d), b.reshape(1, d), router_w_t, router_b.reshape(N_EXPERTS, 1))


_PAIR_ORDER = ((0, 1), (0, 2), (1, 2), (1, 3), (0, 3), (2, 3))
_PAIR_OF_CODE = {**{1 << a: next(i for i, p in enumerate(_PAIR_ORDER) if a in p) for a in range(EXPERTS_PER_GROUP)},
                 **{(1 << a) | (1 << b): i for i, (a, b) in enumerate(_PAIR_ORDER)}}


def _moe_kernel(cnt_ref, tok_ref, need_ref, expert_ref, x_hbm, gate_ref, wg_ref, wu_ref, wd_ref, lng_ref,
                lnb_ref, out_hbm, xg_ref, xb_ref, acc_ref, sem_in, sem_out, *, n_tokens, alpha):
    del expert_ref
    _, tm, d = xg_ref.shape
    nb = pl.num_programs(0)
    blk = pl.program_id(0)
    e = pl.program_id(1)
    n_valid = cnt_ref[blk]
    active = n_valid > 0
    slot = blk % 2
    other = 1 - slot
    next_active = (blk + 1 < nb) & (cnt_ref[jnp.minimum(blk + 1, nb - 1)] > 0)

    def start_gather(b, s):
        def issue(r, carry):
            src = jnp.minimum(tok_ref[b * tm + r], n_tokens - 1)
            pltpu.make_async_copy(x_hbm.at[pl.ds(src, 1)], xg_ref.at[s, pl.ds(r, 1)], sem_in.at[s]).start()
            return carry
        lax.fori_loop(0, tm, issue, 0, unroll=8)

    def wait_gather(s):
        pltpu.make_async_copy(x_hbm.at[pl.ds(0, tm)], xg_ref.at[s], sem_in.at[s]).wait()

    def start_scatter(b, s, rows):
        def put(r, carry):
            tok = tok_ref[b * tm + r]
            pltpu.make_async_copy(xg_ref.at[s, pl.ds(r, 1)], out_hbm.at[pl.ds(tok, 1)], sem_out.at[s]).start()
            return carry
        lax.fori_loop(0, rows, put, 0)

    def wait_scatter(s, rows):
        def drain(r, carry):
            pltpu.make_async_copy(xg_ref.at[s, pl.ds(0, 1)], out_hbm.at[pl.ds(0, 1)], sem_out.at[s]).wait()
            return carry
        lax.fori_loop(0, rows, drain, 0)

    @pl.when(active & (e == 0))
    def _():
        @pl.when(blk == 0)
        def _():
            start_gather(0, 0)
        wait_gather(slot)
        xb_ref[...] = xg_ref[slot].astype(BF16)
        acc_ref[...] = jnp.zeros(acc_ref.shape, F32)

    @pl.when(active & (e == 1))
    def _():
        @pl.when(blk > 0)
        def _():
            wait_scatter(other, cnt_ref[jnp.maximum(blk - 1, 0)])

        @pl.when(next_active)
        def _():
            start_gather(blk + 1, other)

    @pl.when(active & (need_ref[blk * EXPERTS_PER_GROUP + e] != 0))
    def _():
        x = xb_ref[...]
        hg = jnp.dot(x, wg_ref[...], preferred_element_type=F32)
        hu = jnp.dot(x, wu_ref[...], preferred_element_type=F32)
        hm = (hg * _sigmoid(hg) * hu).astype(BF16)
        gates = gate_ref[...]
        gate = gates[:, EXPERTS_PER_GROUP - 1:EXPERTS_PER_GROUP]
        for j in range(EXPERTS_PER_GROUP - 2, -1, -1):
            gate = jnp.where(e == j, gates[:, j:j + 1], gate)
        cols = min(MOE_COL_CHUNK, d)
        for c in range(d // cols):
            sl = slice(c * cols, (c + 1) * cols)
            acc_ref[:, sl] += jnp.dot(hm, wd_ref[:, sl], preferred_element_type=F32) * gate

    @pl.when(active & (e == EXPERTS_PER_GROUP - 1))
    def _():
        xg_ref[slot] = _layer_norm_rows(alpha * xg_ref[slot] + acc_ref[...], lng_ref[...], lnb_ref[...])
        start_scatter(blk, slot, n_valid)

        @pl.when(jnp.logical_not(next_active))
        def _():
            wait_scatter(slot, n_valid)


def moe_ffn_layer_norm(h, routing, wg, wu, wd, layer, ln_g, ln_b, alpha, tm=512):
    n, d = h.shape
    tm = min(tm, n)
    gates = routing[:EXPERTS_PER_GROUP]
    grp = routing[EXPERTS_PER_GROUP].astype(jnp.int32)

    code = sum((gates[j] > 0).astype(jnp.int32) << j for j in range(EXPERTS_PER_GROUP))
    pair = sum(jnp.where(code == c, p, 0) for c, p in _PAIR_OF_CODE.items())
    n_pairs = len(_PAIR_ORDER)
    n_cls = N_GROUPS * n_pairs
    cls = grp * n_pairs + pair
    onehot = (cls[:, None] == jnp.arange(n_cls, dtype=jnp.int32)[None, :]).astype(jnp.int32)
    csum = jnp.cumsum(onehot, axis=0)
    cls_counts = csum[-1]
    rank = jnp.sum(csum * onehot, axis=1) - 1
    counts = cls_counts.reshape(N_GROUPS, n_pairs).sum(axis=1)
    padded = (counts + tm - 1) // tm * tm
    pends = jnp.cumsum(padded)
    before = (jnp.cumsum(cls_counts) - cls_counts).reshape(N_GROUPS, n_pairs)
    cls_start = ((pends - padded)[:, None] + before - before[:, :1]).reshape(n_cls)
    dest = jnp.sum(onehot * cls_start[None, :], axis=1) + rank
    p_rows = n + N_GROUPS * tm
    nb = p_rows // tm
    payload = jnp.concatenate([gates.T, jnp.arange(n, dtype=F32)[:, None]], axis=1)
    empty = jnp.zeros((p_rows, EXPERTS_PER_GROUP + 1), F32).at[:, EXPERTS_PER_GROUP].set(float(n))
    placed = empty.at[dest].set(payload)
    row_tok = placed[:, EXPERTS_PER_GROUP].astype(jnp.int32)
    gate_sorted = placed[:, :EXPERTS_PER_GROUP]
    blk_start = jnp.arange(nb, dtype=jnp.int32) * tm
    blk_grp = jnp.minimum(jnp.searchsorted(pends, blk_start, side='right'), N_GROUPS - 1).astype(jnp.int32)
    grp_end = (pends - padded + counts)[blk_grp]
    blk_cnt = jnp.where(blk_start < pends[-1], jnp.clip(grp_end - blk_start, 0, tm), 0).astype(jnp.int32)

    n_steps = nb * EXPERTS_PER_GROUP
    need = (gate_sorted.reshape(nb, tm, EXPERTS_PER_GROUP) > 0).any(axis=1).reshape(n_steps)
    expert = (blk_grp[:, None] * EXPERTS_PER_GROUP
              + jnp.arange(EXPERTS_PER_GROUP, dtype=jnp.int32)[None, :]).reshape(n_steps)
    step = jnp.arange(n_steps, dtype=jnp.int32)
    held = lax.cummax(jnp.where(need, step * N_EXPERTS + expert, -1))
    step_expert = jnp.where(held >= 0, held % N_EXPERTS, expert[0]).astype(jnp.int32)
    step_need = need.astype(jnp.int32)

    d_exp = wg.shape[-1]

    def w_idx(blk, e, cnt_ref, tok_ref, need_ref, expert_ref):
        return layer * N_EXPERTS + expert_ref[blk * EXPERTS_PER_GROUP + e]

    grid_spec = pltpu.PrefetchScalarGridSpec(
        num_scalar_prefetch=4,
        grid=(nb, EXPERTS_PER_GROUP),
        in_specs=[pl.BlockSpec(memory_space=pl.ANY),
                  pl.BlockSpec((tm, EXPERTS_PER_GROUP), lambda blk, e, *_: (blk, 0)),
                  pl.BlockSpec((None, d, d_exp), lambda blk, e, *s: (w_idx(blk, e, *s), 0, 0)),
                  pl.BlockSpec((None, d, d_exp), lambda blk, e, *s: (w_idx(blk, e, *s), 0, 0)),
                  pl.BlockSpec((None, d_exp, d), lambda blk, e, *s: (w_idx(blk, e, *s), 0, 0)),
                  pl.BlockSpec((1, d), lambda blk, e, *_: (0, 0)),
                  pl.BlockSpec((1, d), lambda blk, e, *_: (0, 0))],
        out_specs=pl.BlockSpec(memory_space=pl.ANY),
        scratch_shapes=[pltpu.VMEM((2, tm, d), F32), pltpu.VMEM((tm, d), BF16), pltpu.VMEM((tm, d), F32),
                        pltpu.SemaphoreType.DMA((2,)), pltpu.SemaphoreType.DMA((2,))],
    )
    return pl.pallas_call(
        functools.partial(_moe_kernel, n_tokens=n, alpha=alpha),
        grid_spec=grid_spec,
        out_shape=jax.ShapeDtypeStruct((n, d), F32),
        compiler_params=_params("arbitrary", "arbitrary"),
        name="moe_ffn_layer_norm",
    )(blk_cnt, row_tok, step_need, step_expert, h, gate_sorted, wg, wu, wd, ln_g.reshape(1, d), ln_b.reshape(1, d))


def kernel(x, positions, attn_w_in, attn_q_norm, attn_w_uq, attn_w_uq_idx, attn_kidx_norm_g, attn_kidx_norm_b, attn_w_o, hgrn_w_in, hgrn_lower_bounds, hgrn_out_norm, hgrn_w_o, ln_mix_g, ln_mix_b, ln_ffn_g, ln_ffn_b, router_w, router_b, moe_w_gate, moe_w_up, moe_w_down):
    batch, seq, d = x.shape
    n = batch * seq
    depth = ln_mix_g.shape[0]
    alpha = (2 * depth) ** 0.25
    q_rank = attn_w_uq.shape[1]
    n_q = attn_w_uq.shape[2] // HEAD_DIM
    n_kv = n_q // Q_GROUP
    kv_dim = n_kv * HEAD_DIM
    n_idx = attn_w_uq_idx.shape[2] // IDX_DIM
    k_sel = min(INDEX_TOPK, seq // 4)
    att_in = attn_w_in.shape[2]
    att_pad = q_rank + 2 * kv_dim + LANES - att_in
    tk = min(ATTN_BLOCK, seq)

    w_in_a = jnp.pad(attn_w_in, ((0, 0), (0, 0), (0, att_pad))).astype(BF16)
    w_o_a = attn_w_o.astype(BF16)
    w_in_h = hgrn_w_in.astype(BF16)
    w_o_h = hgrn_w_o.astype(BF16)
    d_exp = moe_w_gate.shape[-1]
    wg = moe_w_gate.astype(BF16).reshape(depth * N_EXPERTS, d, d_exp)
    wu = moe_w_up.astype(BF16).reshape(depth * N_EXPERTS, d, d_exp)
    wd = moe_w_down.astype(BF16).reshape(depth * N_EXPERTS, d_exp, d)
    router_w_t = router_w.T.astype(BF16)
    lb_soft = jax.nn.softmax(hgrn_lower_bounds.astype(F32), axis=0)
    lower_bounds = jnp.cumsum(lb_soft, axis=0) - lb_soft[0]

    c128, s128, c64, s64, ct128, st128, ct64, st64 = rope_tables(positions)

    h = x.reshape(n, d)
    for layer in range(depth):
        j = layer // 2
        if layer % 2 == 0:
            proj = matmul(h, w_in_a, j, tn=640)
            q_t = query_projection_t(proj, q_rank, attn_q_norm[j], attn_w_uq[j].T.astype(BF16), ct128, st128,
                                     HEAD_DIM, HEAD_DIM ** -0.5 * LOG2E)
            q_idx_t = query_projection_t(proj, q_rank, attn_q_norm[j], attn_w_uq_idx[j].T.astype(BF16), ct64,
                                         st64, IDX_DIM, 1.0)
            k, v_t, k_a, k_b, w_t = key_value_projection(
                proj, q_rank, kv_dim, attn_kidx_norm_g[j], attn_kidx_norm_b[j], (c128, s128, c64, s64),
                n_idx ** -0.5 * IDX_DIM ** -0.5, tk)
            bias = indexer_bias(q_idx_t, w_t, k_a, k_b, batch, seq, n_idx, k_sel, tk)
            o = masked_attention(q_t, k, v_t, bias, batch, seq, n_kv)
            mix = matmul(o, w_o_a, j)
        else:
            proj = matmul(h, w_in_h, j)
            o = hgrn_recurrence(proj, lower_bounds[layer], hgrn_out_norm[j], batch, seq, d)
            mix = matmul(o, w_o_h, j)
        h, routing = residual_layer_norm_route(h, mix, ln_mix_g[layer], ln_mix_b[layer], alpha,
                                               router_w_t, router_b)
        h = moe_ffn_layer_norm(h, routing, wg, wu, wd, layer, ln_ffn_g[layer], ln_ffn_b[layer], alpha)
    return h.reshape(batch, seq, d)
```

```python
import functools
import math

import jax
import jax.numpy as jnp
from jax import lax
from jax.experimental import pallas as pl
from jax.experimental.pallas import tpu as pltpu

F32 = jnp.float32
BF16 = jnp.bfloat16

HEAD_DIM = 128
Q_GROUP = 4
IDX_DIM = 64
INDEX_TOPK = 256
ROPE_THETA = 10000.0
HGRN_EXPAND = 128
HGRN_CHUNK = 64
ATTN_BLOCK = 512
V_ROWS = HEAD_DIM + 16
N_EXPERTS = 32
N_GROUPS = 8
EXPERTS_PER_GROUP = N_EXPERTS // N_GROUPS
MOE_COL_CHUNK = 1024
ROW_DMA_UNROLL = 8
NORM_EPS = 1e-5

LANES = 128
SUBLANES = 8
VMEM_LIMIT_BYTES = 56 * 1024 * 1024

NEG_BIAS = -1e30
INT_MIN = -(2 ** 31)
LOG2E = 1.4426950408889634
SAFE_DECAY_EXPONENT = 80.0

_NT = (((1,), (1,)), ((), ()))
_TN = (((0,), (0,)), ((), ()))


def _pick_tile(n, target, quantum=LANES):
    if n <= target:
        return n
    t = (target // quantum) * quantum
    while t > quantum and n % t:
        t -= quantum
    assert n % t == 0, (n, target)
    return t


def _params(*sem):
    return pltpu.CompilerParams(dimension_semantics=sem, vmem_limit_bytes=VMEM_LIMIT_BYTES)


def _sigmoid(x):
    return 1.0 / (1.0 + jnp.exp(-x))


def _mm_kernel(a_ref, b_ref, o_ref, abf_ref):
    @pl.when(pl.program_id(1) == 0)
    def _():
        abf_ref[...] = a_ref[...].astype(BF16)

    o_ref[...] = jnp.dot(abf_ref[...], b_ref[...], preferred_element_type=F32).astype(o_ref.dtype)


def matmul(a, b, layer, out_dtype=F32, tm=512, tn=1024):
    m, k = a.shape
    n = b.shape[2]
    tm = _pick_tile(m, tm, 8)
    tn = _pick_tile(n, tn)
    return pl.pallas_call(
        _mm_kernel,
        grid=(m // tm, n // tn),
        in_specs=[pl.BlockSpec((tm, k), lambda i, j: (i, 0)),
                  pl.BlockSpec((None, k, tn), lambda i, j: (layer, 0, j))],
        out_specs=pl.BlockSpec((tm, tn), lambda i, j: (i, j)),
        out_shape=jax.ShapeDtypeStruct((m, n), out_dtype),
        scratch_shapes=[pltpu.VMEM((tm, k), BF16)],
        compiler_params=_params("parallel", "arbitrary"),
        name="matmul",
    )(a, b)


def _layer_norm_rows(z, g, b):
    mu = jnp.mean(z, axis=-1, keepdims=True)
    zc = z - mu
    var = jnp.mean(zc * zc, axis=-1, keepdims=True)
    return zc * lax.rsqrt(var + NORM_EPS) * g + b


def _rope_tab_kernel(pc_ref, pr_ref, f128_ref, f64_ref, fc128_ref, fc64_ref,
                     c128_ref, s128_ref, c64_ref, s64_ref, ct128_ref, st128_ref, ct64_ref, st64_ref):
    pos_col = pc_ref[...]
    pos_row = pr_ref[...]
    lane = lax.broadcasted_iota(jnp.int32, (1, LANES), 1)
    for hd, f_ref, c_ref, s_ref in ((HEAD_DIM, f128_ref, c128_ref, s128_ref),
                                    (IDX_DIM, f64_ref, c64_ref, s64_ref)):
        ang = pos_col * f_ref[...]
        sign = jnp.where((lane % hd) < hd // 2, -1.0, 1.0)
        c_ref[...] = jnp.cos(ang)
        s_ref[...] = jnp.sin(ang) * sign
    for fc_ref, c_ref, s_ref in ((fc128_ref, ct128_ref, st128_ref), (fc64_ref, ct64_ref, st64_ref)):
        ang = fc_ref[...] * pos_row
        c_ref[...] = jnp.cos(ang)
        s_ref[...] = jnp.sin(ang)


def rope_tables(positions):
    n = positions.size
    tm = _pick_tile(n, 512)
    pos = positions.reshape(n).astype(F32)

    def inv_freq(hd):
        return ROPE_THETA ** (-2.0 * jnp.arange(hd // 2, dtype=F32) / hd)

    def lanes(hd):
        return jnp.tile(inv_freq(hd), LANES // (hd // 2)).reshape(1, LANES)

    h128, h64 = HEAD_DIM // 2, IDX_DIM // 2
    tab = pl.BlockSpec((tm, LANES), lambda i: (i, 0))
    vec = pl.BlockSpec((1, LANES), lambda i: (0, 0))
    tok = jax.ShapeDtypeStruct((n, LANES), F32)
    return pl.pallas_call(
        _rope_tab_kernel,
        grid=(n // tm,),
        in_specs=[pl.BlockSpec((tm, 1), lambda i: (i, 0)), pl.BlockSpec((1, tm), lambda i: (0, i)), vec, vec,
                  pl.BlockSpec((h128, 1), lambda i: (0, 0)), pl.BlockSpec((h64, 1), lambda i: (0, 0))],
        out_specs=[tab, tab, tab, tab,
                   pl.BlockSpec((h128, tm), lambda i: (0, i)), pl.BlockSpec((h128, tm), lambda i: (0, i)),
                   pl.BlockSpec((h64, tm), lambda i: (0, i)), pl.BlockSpec((h64, tm), lambda i: (0, i))],
        out_shape=[tok, tok, tok, tok,
                   jax.ShapeDtypeStruct((h128, n), F32), jax.ShapeDtypeStruct((h128, n), F32),
                   jax.ShapeDtypeStruct((h64, n), F32), jax.ShapeDtypeStruct((h64, n), F32)],
        compiler_params=_params("parallel"),
        name="rope_tables",
    )(pos.reshape(n, 1), pos.reshape(1, n), lanes(HEAD_DIM), lanes(IDX_DIM),
      inv_freq(HEAD_DIM).reshape(h128, 1), inv_freq(IDX_DIM).reshape(h64, 1))


def _rotate_half(x, hd):
    if hd == LANES:
        return pltpu.roll(x, LANES // 2, 1)
    lane = lax.broadcasted_iota(jnp.int32, x.shape, 1)
    half = hd // 2
    return jnp.where((lane % hd) < half, pltpu.roll(x, LANES - half, 1), pltpu.roll(x, half, 1))


def _qprep_kernel(cq_ref, g_ref, wt_ref, cos_ref, sin_ref, o_ref, xn_ref, *, hd, scale):
    @pl.when(pl.program_id(1) == 0)
    def _():
        x = cq_ref[...]
        ms = jnp.mean(x * x, axis=-1, keepdims=True)
        xn_ref[...] = (x * lax.rsqrt(ms + NORM_EPS) * g_ref[...]).astype(BF16)

    y = lax.dot_general(wt_ref[...], xn_ref[...], _NT, preferred_element_type=F32)
    cos = cos_ref[...]
    sin = sin_ref[...]
    half = hd // 2
    for hh in range(y.shape[0] // hd):
        lo = hh * hd
        x1 = y[lo:lo + half]
        x2 = y[lo + half:lo + hd]
        o_ref[lo:lo + half, :] = ((x1 * cos - x2 * sin) * scale).astype(o_ref.dtype)
        o_ref[lo + half:lo + hd, :] = ((x2 * cos + x1 * sin) * scale).astype(o_ref.dtype)


def query_projection_t(proj, q_rank, q_norm, w_t, cos_t, sin_t, hd, scale, tm=512, tn=1024):
    n = proj.shape[0]
    nout = w_t.shape[0]
    tm = _pick_tile(n, tm)
    tn = _pick_tile(nout, tn)
    tab = pl.BlockSpec((hd // 2, tm), lambda i, j: (0, i))
    return pl.pallas_call(
        functools.partial(_qprep_kernel, hd=hd, scale=scale),
        grid=(n // tm, nout // tn),
        in_specs=[pl.BlockSpec((tm, q_rank), lambda i, j: (i, 0)),
                  pl.BlockSpec((1, q_rank), lambda i, j: (0, 0)),
                  pl.BlockSpec((tn, q_rank), lambda i, j: (j, 0)),
                  tab, tab],
        out_specs=pl.BlockSpec((tn, tm), lambda i, j: (j, i)),
        out_shape=jax.ShapeDtypeStruct((nout, n), BF16),
        scratch_shapes=[pltpu.VMEM((tm, q_rank), BF16)],
        compiler_params=_params("parallel", "arbitrary"),
        name="query_projection",
    )(proj, q_norm.reshape(1, q_rank), w_t, cos_t, sin_t)


def _kprep_kernel(k_ref, v_ref, t_ref, g_ref, b_ref, c128_ref, s128_ref, c64_ref, s64_ref,
                  ko_ref, vt_ref, ka_ref, kb_ref, wt_ref, *, w_scale):
    cos = c128_ref[...]
    sin = s128_ref[...]
    for c in range(k_ref.shape[1] // LANES):
        sl = slice(c * LANES, (c + 1) * LANES)
        x = k_ref[:, sl]
        ko_ref[:, sl] = (x * cos + _rotate_half(x, HEAD_DIM) * sin).astype(BF16)
        vt_ref[c, :HEAD_DIM, :] = v_ref[:, sl].T.astype(BF16)
        vt_ref[c, HEAD_DIM:, :] = jnp.ones((V_ROWS - HEAD_DIM, v_ref.shape[0]), BF16)

    t = t_ref[...]
    lane = lax.broadcasted_iota(jnp.int32, t.shape, 1)
    is_key = lane < IDX_DIM
    mu = jnp.sum(jnp.where(is_key, t, 0.0), axis=-1, keepdims=True) * (1.0 / IDX_DIM)
    tc = jnp.where(is_key, t - mu, 0.0)
    var = jnp.sum(tc * tc, axis=-1, keepdims=True) * (1.0 / IDX_DIM)
    y = tc * lax.rsqrt(var + NORM_EPS) * g_ref[...] + b_ref[...]
    y = y * c64_ref[...] + _rotate_half(y, IDX_DIM) * s64_ref[...]
    ka_ref[...] = y.astype(BF16)
    kb_ref[...] = pltpu.roll(y, IDX_DIM, 1).astype(BF16)
    wt_ref[...] = t.T[IDX_DIM:, :] * w_scale


def key_value_projection(proj, q_rank, kv_dim, kidx_g, kidx_b, tabs, w_scale, tk):
    n = proj.shape[0]
    assert q_rank % kv_dim == 0 and (q_rank + 2 * kv_dim) % LANES == 0
    kb0 = q_rank // kv_dim
    tail_blk = (q_rank + 2 * kv_dim) // LANES
    n_kv = kv_dim // HEAD_DIM
    pad = LANES - IDX_DIM
    g = jnp.pad(kidx_g, (0, pad)).reshape(1, LANES)
    b = jnp.pad(kidx_b, (0, pad)).reshape(1, LANES)
    tab = pl.BlockSpec((tk, LANES), lambda i: (i, 0))
    vec = pl.BlockSpec((1, LANES), lambda i: (0, 0))
    return pl.pallas_call(
        functools.partial(_kprep_kernel, w_scale=w_scale),
        grid=(n // tk,),
        in_specs=[pl.BlockSpec((tk, kv_dim), lambda i: (i, kb0)),
                  pl.BlockSpec((tk, kv_dim), lambda i: (i, kb0 + 1)),
                  pl.BlockSpec((tk, LANES), lambda i: (i, tail_blk)),
                  vec, vec, tab, tab, tab, tab],
        out_specs=[pl.BlockSpec((tk, kv_dim), lambda i: (i, 0)),
                   pl.BlockSpec((None, n_kv, V_ROWS, tk), lambda i: (i, 0, 0, 0)),
                   tab, tab,
                   pl.BlockSpec((LANES - IDX_DIM, tk), lambda i: (0, i))],
        out_shape=[jax.ShapeDtypeStruct((n, kv_dim), BF16),
                   jax.ShapeDtypeStruct((n // tk, n_kv, V_ROWS, tk), BF16),
                   jax.ShapeDtypeStruct((n, LANES), BF16), jax.ShapeDtypeStruct((n, LANES), BF16),
                   jax.ShapeDtypeStruct((LANES - IDX_DIM, n), F32)],
        compiler_params=_params("parallel"),
        name="key_value_projection",
    )(proj, proj, proj, g, b, *tabs)


def _indexer_kernel(q_ref, w_ref, ka_ref, kb_ref, o_ref, key_ref, cut_ref, *, n_heads, k_sel, seq):
    nk, tk, tq = key_ref.shape
    qi = pl.program_id(1)
    n_live = qi + 1
    key_idx0 = lax.broadcasted_iota(jnp.int32, (tk, tq), 0)
    q_idx = lax.broadcasted_iota(jnp.int32, (tk, tq), 1) + qi * tq

    def score_block(kj, carry):
        off = pl.multiple_of(kj * tk, tk)
        ka = ka_ref[pl.ds(off, tk), :]
        kb = kb_ref[pl.ds(off, tk), :]
        acc = jnp.zeros((tk, tq), F32)
        for p in range(n_heads // 2):
            qp = q_ref[p * LANES:(p + 1) * LANES, :]
            la = jnp.dot(ka, qp, preferred_element_type=F32)
            lb = jnp.dot(kb, qp, preferred_element_type=F32)
            acc = acc + jnp.maximum(la, 0.0) * w_ref[2 * p:2 * p + 1, :]
            acc = acc + jnp.maximum(lb, 0.0) * w_ref[2 * p + 1:2 * p + 2, :]
        s = jnp.where(key_idx0 + kj * tk <= q_idx, acc, -jnp.inf)
        bits = pltpu.bitcast(s, jnp.int32)
        key_ref[kj] = bits ^ ((bits >> 31) & 0x7FFFFFFF)
        return carry

    lax.fori_loop(0, n_live, score_block, 0)

    sub = lax.broadcasted_iota(jnp.int32, (SUBLANES, tq), 0)

    def count(pred):
        def body(kj, c):
            for r in range(tk // SUBLANES):
                key = key_ref[kj, r * SUBLANES:(r + 1) * SUBLANES, :]
                idx = sub + (kj * tk + r * SUBLANES)
                c = c + jnp.where(pred(key, idx), 1.0, 0.0)
            return c
        c = lax.fori_loop(0, n_live, body, jnp.zeros((SUBLANES, tq), F32))
        return jnp.broadcast_to(jnp.sum(c, axis=0, keepdims=True), (SUBLANES, tq))

    def bit_step(i, carry):
        u, n_u = carry
        cand = u | lax.shift_left(jnp.int32(1), 31 - i)
        cand_signed = cand ^ INT_MIN
        cnt = count(lambda key, idx: key >= cand_signed)
        take = cnt >= k_sel
        return jnp.where(take, cand, u), jnp.where(take, cnt, n_u)

    n_all = jnp.zeros((SUBLANES, tq), F32) + (n_live * tk).astype(F32)
    u, n_ge = lax.fori_loop(0, 32, bit_step, (jnp.zeros((SUBLANES, tq), jnp.int32), n_all))
    thr = u ^ INT_MIN

    cut_ref[...] = jnp.full((SUBLANES, tq), seq, jnp.int32)

    @pl.when(jnp.max(n_ge) > k_sel)
    def _():
        need = k_sel - count(lambda key, idx: key > thr)
        n_bits = max(1, (seq - 1).bit_length())

        def idx_step(i, p):
            t = p + lax.shift_left(jnp.int32(1), n_bits - 1 - i)
            cnt = count(lambda key, idx: (key == thr) & (idx < t))
            return jnp.where(cnt < need, t, p)

        cut_ref[...] = lax.fori_loop(0, n_bits, idx_step, jnp.zeros((SUBLANES, tq), jnp.int32))

    thr_b = jnp.broadcast_to(thr[:1], (tk, tq))
    cut_b = jnp.broadcast_to(cut_ref[:1, :], (tk, tq))

    def emit(kj, carry):
        key = key_ref[kj]
        idx = key_idx0 + kj * tk
        keep = (key > thr_b) | ((key == thr_b) & (idx <= cut_b))
        keep = keep & (idx <= q_idx)
        o_ref[kj] = jnp.where(keep, 0.0, NEG_BIAS).astype(BF16)
        return carry

    lax.fori_loop(0, n_live, emit, 0)

    def emit_dead(kj, carry):
        o_ref[kj] = jnp.full((tk, tq), NEG_BIAS, BF16)
        return carry

    lax.fori_loop(n_live, nk, emit_dead, 0)


def indexer_bias(q_idx_t, w_t, k_a, k_b, batch, seq, n_heads, k_sel, tq):
    nq = seq // tq
    nk = nq
    w_rows = -(-n_heads // SUBLANES) * SUBLANES
    kspec = pl.BlockSpec((seq, LANES), lambda b, i: (b, 0))
    return pl.pallas_call(
        functools.partial(_indexer_kernel, n_heads=n_heads, k_sel=k_sel, seq=seq),
        grid=(batch, nq),
        in_specs=[pl.BlockSpec((n_heads * IDX_DIM, tq), lambda b, i: (0, b * nq + i)),
                  pl.BlockSpec((w_rows, tq), lambda b, i: (0, b * nq + i)),
                  kspec, kspec],
        out_specs=pl.BlockSpec((None, nk, tq, tq), lambda b, i: (b * nq + i, 0, 0, 0)),
        out_shape=jax.ShapeDtypeStruct((batch * nq, nk, tq, tq), BF16),
        scratch_shapes=[pltpu.VMEM((nk, tq, tq), jnp.int32), pltpu.VMEM((SUBLANES, tq), jnp.int32)],
        compiler_params=_params("parallel", "arbitrary"),
        name="indexer_bias",
    )(q_idx_t, w_t, k_a, k_b)


def _attn_kernel(q_ref, k_ref, v_ref, b_ref, o_ref, qcat_ref, sa_ref, sb_ref, m_ref, acc_ref):
    _, tk, tq = b_ref.shape
    qi = pl.program_id(2)
    for h in range(Q_GROUP):
        qcat_ref[:, h * tq:(h + 1) * tq] = q_ref[h * HEAD_DIM:(h + 1) * HEAD_DIM, :]
    m_ref[...] = jnp.full(m_ref.shape, NEG_BIAS, F32)
    acc_ref[...] = jnp.zeros(acc_ref.shape, F32)

    def scores(kj):
        k = k_ref[pl.ds(pl.multiple_of(kj * tk, tk), tk), :]
        return jnp.dot(k, qcat_ref[...], preferred_element_type=F32)

    def accumulate(kj, s):
        bias = b_ref[kj].astype(F32)
        s = s + jnp.concatenate([bias] * Q_GROUP, axis=1)
        m_prev = m_ref[...]
        m_new = jnp.maximum(m_prev, jnp.max(s, axis=0, keepdims=True))
        alpha = jnp.exp2(m_prev - m_new)
        p = jnp.exp2(s - m_new).astype(BF16)
        acc_ref[...] = alpha * acc_ref[...] + jnp.dot(v_ref[kj], p, preferred_element_type=F32)
        m_ref[...] = m_new

    n_blocks = qi + 1
    sa_ref[...] = scores(0)

    def pair(i, carry):
        kj = 2 * i
        sb_ref[...] = scores(kj + 1)
        accumulate(kj, sa_ref[...])
        sa_ref[...] = scores(jnp.minimum(kj + 2, qi))
        accumulate(kj + 1, sb_ref[...])
        return carry

    lax.fori_loop(0, n_blocks // 2, pair, 0)

    @pl.when(n_blocks % 2 == 1)
    def _():
        accumulate(qi, sa_ref[...])

    for h in range(Q_GROUP):
        sl = slice(h * tq, (h + 1) * tq)
        out = acc_ref[:HEAD_DIM, sl] / acc_ref[HEAD_DIM:HEAD_DIM + 1, sl]
        o_ref[:, h * HEAD_DIM:(h + 1) * HEAD_DIM] = out.T.astype(o_ref.dtype)


def masked_attention(q_t, k, v_t, bias, batch, seq, n_kv):
    _, nk, tk, tq = bias.shape
    nq = seq // tq
    n = batch * seq
    gw = Q_GROUP * HEAD_DIM
    v5 = v_t.reshape(batch, nk, n_kv, V_ROWS, tk)
    return pl.pallas_call(
        _attn_kernel,
        grid=(batch, n_kv, nq),
        in_specs=[pl.BlockSpec((gw, tq), lambda b, g, i: (g, b * nq + i)),
                  pl.BlockSpec((seq, HEAD_DIM), lambda b, g, i: (b, g)),
                  pl.BlockSpec((None, nk, None, V_ROWS, tk), lambda b, g, i: (b, 0, g, 0, 0)),
                  pl.BlockSpec((None, nk, tk, tq), lambda b, g, i: (b * nq + i, 0, 0, 0))],
        out_specs=pl.BlockSpec((tq, gw), lambda b, g, i: (b * nq + i, g)),
        out_shape=jax.ShapeDtypeStruct((n, n_kv * gw), BF16),
        scratch_shapes=[pltpu.VMEM((HEAD_DIM, Q_GROUP * tq), BF16),
                        pltpu.VMEM((tk, Q_GROUP * tq), F32), pltpu.VMEM((tk, Q_GROUP * tq), F32),
                        pltpu.VMEM((1, Q_GROUP * tq), F32),
                        pltpu.VMEM((V_ROWS, Q_GROUP * tq), F32)],
        compiler_params=_params("parallel", "parallel", "arbitrary"),
        name="masked_attention",
    )(q_t, k, v5, bias)


def _hgrn_kernel(q_ref, f_ref, i_ref, g_ref, lb_ref, on_ref, o_ref, st_ref, intra_ref, mask_ref, *, chunk):
    tb = q_ref.shape[0]
    n_heads = q_ref.shape[1] // HGRN_EXPAND
    c_rows = chunk

    @pl.when(pl.program_id(2) == 0)
    def _():
        st_ref[...] = jnp.zeros(st_ref.shape, F32)
        rr = lax.broadcasted_iota(jnp.int32, mask_ref.shape, 0)
        cc = lax.broadcasted_iota(jnp.int32, mask_ref.shape, 1)
        mask_ref[...] = jnp.where((cc <= rr) & (cc >= rr - rr % c_rows), 1.0, 0.0)

    rowv = lax.broadcasted_iota(jnp.int32, (c_rows, HGRN_EXPAND), 0)
    onorm = on_ref[...]

    def do_chunk(c, carry):
        rows = pl.ds(pl.multiple_of(c * c_rows, c_rows), c_rows)

        heads = []
        for h in range(n_heads):
            sl = slice(h * HGRN_EXPAND, (h + 1) * HGRN_EXPAND)
            qr = q_ref[rows, sl]
            fr = f_ref[rows, sl]
            lb = lb_ref[:, sl]
            q = qr * _sigmoid(qr)
            e = jnp.exp(-jnp.abs(fr))
            r = 1.0 / (1.0 + e)
            sig_pos = jnp.where(fr >= 0, r, e * r)
            sig_neg = jnp.where(fr >= 0, e * r, r)
            log_f = jnp.log(lb + (1.0 - lb) * sig_pos)
            kk = (1.0 - lb) * sig_neg
            b = log_f
            s = 1
            while s < c_rows:
                b = b + jnp.where(rowv >= s, pltpu.roll(b, s, 0), 0.0)
                s *= 2
            b_last = b[c_rows - 1:c_rows, :]
            heads.append(dict(sl=sl, q=q, kk=kk, b=b, b_last=b_last,
                              qt=(q * jnp.exp(b)).astype(BF16), v=i_ref[rows, sl].astype(BF16)))

        b_min = heads[0]["b_last"]
        for hd in heads[1:]:
            b_min = jnp.minimum(b_min, hd["b_last"])
        safe = jnp.min(b_min) >= -SAFE_DECAY_EXPONENT

        @pl.when(safe)
        def _():
            qt_all = jnp.concatenate([hd["qt"] for hd in heads], axis=0)
            kh_all = jnp.concatenate([(hd["kk"] * jnp.exp(-hd["b"])).astype(BF16) for hd in heads], axis=0)
            v_all = jnp.concatenate([hd["v"] for hd in heads], axis=0)
            a = lax.dot_general(qt_all, kh_all, _NT, preferred_element_type=F32)
            a = jnp.where(mask_ref[...] != 0.0, a, 0.0).astype(BF16)
            o_all = jnp.dot(a, v_all, preferred_element_type=F32)
            for h in range(n_heads):
                intra_ref[h] = o_all[h * c_rows:(h + 1) * c_rows]

        @pl.when(jnp.logical_not(safe))
        def _():
            for h, hd in enumerate(heads):
                q, kk, b = hd["q"], hd["kk"], hd["b"]
                v = i_ref[rows, hd["sl"]]

                def off_step(d, o):
                    kd = pltpu.roll(kk, d, 0)
                    bd = pltpu.roll(b, d, 0)
                    vd = pltpu.roll(v, d, 0)
                    wgt = q * kd * jnp.exp(jnp.minimum(b - bd, 0.0))
                    a = jnp.sum(wgt, axis=1, keepdims=True)
                    return o + jnp.where(rowv >= d, a, 0.0) * vd
                intra_ref[h] = lax.fori_loop(0, c_rows, off_step, jnp.zeros((c_rows, HGRN_EXPAND), F32))

        for h, hd in enumerate(heads):
            st = st_ref[h]
            o = intra_ref[h] + lax.dot_general(hd["qt"], st.astype(BF16), _NT, preferred_element_type=F32)
            k_new = (hd["kk"] * jnp.exp(hd["b_last"] - hd["b"])).astype(BF16)
            st_ref[h] = st * jnp.exp(hd["b_last"]) + lax.dot_general(hd["v"], k_new, _TN,
                                                                      preferred_element_type=F32)
            gr = g_ref[rows, hd["sl"]]
            ms = jnp.mean(o * o, axis=-1, keepdims=True)
            out = o * lax.rsqrt(ms + NORM_EPS) * onorm * (gr * _sigmoid(gr))
            o_ref[rows, hd["sl"]] = out.astype(o_ref.dtype)
        return carry

    lax.fori_loop(0, tb // c_rows, do_chunk, 0)


def hgrn_recurrence(proj, lower_bound, out_norm, batch, seq, d_model, tb=512, heads_per_step=8):
    n = proj.shape[0]
    tb = min(tb, seq)
    nt = seq // tb
    width = min(heads_per_step * HGRN_EXPAND, d_model)
    ng = d_model // width
    chunk = min(HGRN_CHUNK, tb)

    def part(p):
        return pl.BlockSpec((tb, width), lambda b, g, t: (b * nt + t, p * ng + g))

    return pl.pallas_call(
        functools.partial(_hgrn_kernel, chunk=chunk),
        grid=(batch, ng, nt),
        in_specs=[part(0), part(1), part(2), part(3),
                  pl.BlockSpec((1, width), lambda b, g, t: (0, g)),
                  pl.BlockSpec((1, HGRN_EXPAND), lambda b, g, t: (0, 0))],
        out_specs=pl.BlockSpec((tb, width), lambda b, g, t: (b * nt + t, g)),
        out_shape=jax.ShapeDtypeStruct((n, d_model), BF16),
        scratch_shapes=[pltpu.VMEM((width // HGRN_EXPAND, HGRN_EXPAND, HGRN_EXPAND), F32),
                        pltpu.VMEM((width // HGRN_EXPAND, chunk, HGRN_EXPAND), F32),
                        pltpu.VMEM((width // HGRN_EXPAND * chunk, width // HGRN_EXPAND * chunk), F32)],
        compiler_params=_params("parallel", "parallel", "arbitrary"),
        name="hgrn_recurrence",
    )(proj, proj, proj, proj, lower_bound.reshape(1, d_model), out_norm.reshape(1, HGRN_EXPAND))


def _routing(h, router_w_t, router_b):
    logits = lax.dot_general(router_w_t, h.astype(BF16), _NT, preferred_element_type=F32)
    s_all = _sigmoid(logits)
    r_all = s_all + router_b
    tm = logits.shape[1]

    def rows(x, g):
        return [x[g * EXPERTS_PER_GROUP + j:g * EXPERTS_PER_GROUP + j + 1, :] for j in range(EXPERTS_PER_GROUP)]

    best_score = None
    for g in range(N_GROUPS):
        a, b, c, d = rows(r_all, g)
        hi_ab, lo_ab = jnp.maximum(a, b), jnp.minimum(a, b)
        hi_cd, lo_cd = jnp.maximum(c, d), jnp.minimum(c, d)
        top1 = jnp.maximum(hi_ab, hi_cd)
        top2 = jnp.maximum(jnp.minimum(hi_ab, hi_cd), jnp.maximum(lo_ab, lo_cd))
        score = top1 + top2
        if g == 0:
            best_score, best = score, jnp.zeros((1, tm), F32)
            best_r, best_s = rows(r_all, 0), rows(s_all, 0)
        else:
            upd = score > best_score
            best_score = jnp.where(upd, score, best_score)
            best = jnp.where(upd, float(g), best)
            best_r = [jnp.where(upd, n, o) for n, o in zip(rows(r_all, g), best_r)]
            best_s = [jnp.where(upd, n, o) for n, o in zip(rows(s_all, g), best_s)]

    picked = []
    for j in range(EXPERTS_PER_GROUP):
        rank = jnp.zeros((1, tm), F32)
        for i in range(EXPERTS_PER_GROUP):
            if i == j:
                continue
            ahead = (best_r[i] > best_r[j]) | ((best_r[i] == best_r[j]) & (i < j))
            rank = rank + jnp.where(ahead, 1.0, 0.0)
        picked.append(jnp.where(rank < 2.0, best_s[j], 0.0))
    total = picked[0] + picked[1] + picked[2] + picked[3]
    out_rows = [p / total for p in picked] + [best] + [jnp.zeros((1, tm), F32)] * 3
    return jnp.concatenate(out_rows, axis=0)


def _res_ln_route_kernel(h_ref, y_ref, g_ref, b_ref, rw_ref, rb_ref, o_ref, r_ref, *, alpha):
    hn = _layer_norm_rows(alpha * h_ref[...] + y_ref[...], g_ref[...], b_ref[...])
    o_ref[...] = hn
    r_ref[...] = _routing(hn, rw_ref[...], rb_ref[...])


def residual_layer_norm_route(h, y, g, b, alpha, router_w_t, router_b, tm=256):
    n, d = h.shape
    tm = _pick_tile(n, tm)
    row = pl.BlockSpec((tm, d), lambda i: (i, 0))
    vec = pl.BlockSpec((1, d), lambda i: (0, 0))
    return pl.pallas_call(
        functools.partial(_res_ln_route_kernel, alpha=alpha),
        grid=(n // tm,),
        in_specs=[row, row, vec, vec,
                  pl.BlockSpec((N_EXPERTS, d), lambda i: (0, 0)),
                  pl.BlockSpec((N_EXPERTS, 1), lambda i: (0, 0))],
        out_specs=[row, pl.BlockSpec((SUBLANES, tm), lambda i: (0, i))],
        out_shape=[jax.ShapeDtypeStruct((n, d), F32), jax.ShapeDtypeStruct((SUBLANES, n), F32)],
        compiler_params=_params("parallel"),
        name="residual_layer_norm_route",
    )(h, y, g.reshape(1, d), b.reshape(1, d), router_w_t, router_b.reshape(N_EXPERTS, 1))


_PAIR_ORDER = ((0, 1), (0, 2), (1, 2), (1, 3), (0, 3), (2, 3))
_PAIR_OF_CODE = {**{1 << a: next(i for i, p in enumerate(_PAIR_ORDER) if a in p) for a in range(EXPERTS_PER_GROUP)},
                 **{(1 << a) | (1 << b): i for i, (a, b) in enumerate(_PAIR_ORDER)}}


def _moe_kernel(cnt_ref, tok_ref, need_ref, expert_ref, x_hbm, gate_ref, wg_ref, wu_ref, wd_ref, lng_ref,
                lnb_ref, out_hbm, xg_ref, xb_ref, acc_ref, sem_in, sem_out, *, n_tokens, alpha):
    del expert_ref
    _, tm, d = xg_ref.shape
    nb = pl.num_programs(0)
    blk = pl.program_id(0)
    e = pl.program_id(1)
    n_valid = cnt_ref[blk]
    active = n_valid > 0
    slot = blk % 2
    other = 1 - slot
    next_active = (blk + 1 < nb) & (cnt_ref[jnp.minimum(blk + 1, nb - 1)] > 0)

    def start_gather(b, s):
        def issue(i, carry):
            for j in range(ROW_DMA_UNROLL):
                r = i * ROW_DMA_UNROLL + j
                src = jnp.minimum(tok_ref[b * tm + r], n_tokens - 1)
                pltpu.make_async_copy(x_hbm.at[pl.ds(src, 1)], xg_ref.at[s, pl.ds(r, 1)],
                                      sem_in.at[s]).start(priority=j % 2)
            return carry
        lax.fori_loop(0, tm // ROW_DMA_UNROLL, issue, 0)

    def wait_gather(s):
        pltpu.make_async_copy(x_hbm.at[pl.ds(0, tm)], xg_ref.at[s], sem_in.at[s]).wait()

    def start_scatter(b, s, rows):
        def put(r, carry):
            tok = tok_ref[b * tm + r]
            pltpu.make_async_copy(xg_ref.at[s, pl.ds(r, 1)], out_hbm.at[pl.ds(tok, 1)], sem_out.at[s]).start()
            return carry
        lax.fori_loop(0, rows, put, 0)

    def wait_scatter(s, rows):
        def drain(r, carry):
            pltpu.make_async_copy(xg_ref.at[s, pl.ds(0, 1)], out_hbm.at[pl.ds(0, 1)], sem_out.at[s]).wait()
            return carry
        lax.fori_loop(0, rows, drain, 0)

    @pl.when(active & (e == 0))
    def _():
        @pl.when(blk == 0)
        def _():
            start_gather(0, 0)
        wait_gather(slot)
        xb_ref[...] = xg_ref[slot].astype(BF16)
        acc_ref[...] = jnp.zeros(acc_ref.shape, F32)

    @pl.when(active & (e == 1))
    def _():
        @pl.when(blk > 0)
        def _():
            wait_scatter(other, cnt_ref[jnp.maximum(blk - 1, 0)])

        @pl.when(next_active)
        def _():
            start_gather(blk + 1, other)

    @pl.when(active & (need_ref[blk * EXPERTS_PER_GROUP + e] != 0))
    def _():
        x = xb_ref[...]
        hg = jnp.dot(x, wg_ref[...], preferred_element_type=F32)
        hu = jnp.dot(x, wu_ref[...], preferred_element_type=F32)
        hm = (hg * _sigmoid(hg) * hu).astype(BF16)
        gates = gate_ref[...]
        gate = gates[:, EXPERTS_PER_GROUP - 1:EXPERTS_PER_GROUP]
        for j in range(EXPERTS_PER_GROUP - 2, -1, -1):
            gate = jnp.where(e == j, gates[:, j:j + 1], gate)
        cols = min(MOE_COL_CHUNK, d)
        for c in range(d // cols):
            sl = slice(c * cols, (c + 1) * cols)
            acc_ref[:, sl] += jnp.dot(hm, wd_ref[:, sl], preferred_element_type=F32) * gate

    @pl.when(active & (e == EXPERTS_PER_GROUP - 1))
    def _():
        xg_ref[slot] = _layer_norm_rows(alpha * xg_ref[slot] + acc_ref[...], lng_ref[...], lnb_ref[...])
        start_scatter(blk, slot, n_valid)

        @pl.when(jnp.logical_not(next_active))
        def _():
            wait_scatter(slot, n_valid)


def moe_ffn_layer_norm(h, routing, wg, wu, wd, layer, ln_g, ln_b, alpha, tm=512):
    n, d = h.shape
    tm = min(tm, n)
    gates = routing[:EXPERTS_PER_GROUP]
    grp = routing[EXPERTS_PER_GROUP].astype(jnp.int32)

    code = sum((gates[j] > 0).astype(jnp.int32) << j for j in range(EXPERTS_PER_GROUP))
    pair = sum(jnp.where(code == c, p, 0) for c, p in _PAIR_OF_CODE.items())
    n_pairs = len(_PAIR_ORDER)
    n_cls = N_GROUPS * n_pairs
    cls = grp * n_pairs + pair
    onehot = (cls[:, None] == jnp.arange(n_cls, dtype=jnp.int32)[None, :]).astype(jnp.int32)
    csum = jnp.cumsum(onehot, axis=0)
    cls_counts = csum[-1]
    rank = jnp.sum(csum * onehot, axis=1) - 1
    counts = cls_counts.reshape(N_GROUPS, n_pairs).sum(axis=1)
    padded = (counts + tm - 1) // tm * tm
    pends = jnp.cumsum(padded)
    before = (jnp.cumsum(cls_counts) - cls_counts).reshape(N_GROUPS, n_pairs)
    cls_start = ((pends - padded)[:, None] + before - before[:, :1]).reshape(n_cls)
    dest = jnp.sum(onehot * cls_start[None, :], axis=1) + rank
    p_rows = n + N_GROUPS * tm
    nb = p_rows // tm
    payload = jnp.concatenate([gates.T, jnp.arange(n, dtype=F32)[:, None]], axis=1)
    empty = jnp.zeros((p_rows, EXPERTS_PER_GROUP + 1), F32).at[:, EXPERTS_PER_GROUP].set(float(n))
    placed = empty.at[dest].set(payload)
    row_tok = placed[:, EXPERTS_PER_GROUP].astype(jnp.int32)
    gate_sorted = placed[:, :EXPERTS_PER_GROUP]
    blk_start = jnp.arange(nb, dtype=jnp.int32) * tm
    blk_grp = jnp.minimum(jnp.searchsorted(pends, blk_start, side='right'), N_GROUPS - 1).astype(jnp.int32)
    grp_end = (pends - padded + counts)[blk_grp]
    blk_cnt = jnp.where(blk_start < pends[-1], jnp.clip(grp_end - blk_start, 0, tm), 0).astype(jnp.int32)

    n_steps = nb * EXPERTS_PER_GROUP
    need = (gate_sorted.reshape(nb, tm, EXPERTS_PER_GROUP) > 0).any(axis=1).reshape(n_steps)
    expert = (blk_grp[:, None] * EXPERTS_PER_GROUP
              + jnp.arange(EXPERTS_PER_GROUP, dtype=jnp.int32)[None, :]).reshape(n_steps)
    step = jnp.arange(n_steps, dtype=jnp.int32)
    held = lax.cummax(jnp.where(need, step * N_EXPERTS + expert, -1))
    step_expert = jnp.where(held >= 0, held % N_EXPERTS, expert[0]).astype(jnp.int32)
    step_need = need.astype(jnp.int32)

    d_exp = wg.shape[-1]

    def w_idx(blk, e, cnt_ref, tok_ref, need_ref, expert_ref):
        return layer * N_EXPERTS + expert_ref[blk * EXPERTS_PER_GROUP + e]

    grid_spec = pltpu.PrefetchScalarGridSpec(
        num_scalar_prefetch=4,
        grid=(nb, EXPERTS_PER_GROUP),
        in_specs=[pl.BlockSpec(memory_space=pl.ANY),
                  pl.BlockSpec((tm, EXPERTS_PER_GROUP), lambda blk, e, *_: (blk, 0)),
                  pl.BlockSpec((None, d, d_exp), lambda blk, e, *s: (w_idx(blk, e, *s), 0, 0)),
                  pl.BlockSpec((None, d, d_exp), lambda blk, e, *s: (w_idx(blk, e, *s), 0, 0)),
                  pl.BlockSpec((None, d_exp, d), lambda blk, e, *s: (w_idx(blk, e, *s), 0, 0)),
                  pl.BlockSpec((1, d), lambda blk, e, *_: (0, 0)),
                  pl.BlockSpec((1, d), lambda blk, e, *_: (0, 0))],
        out_specs=pl.BlockSpec(memory_space=pl.ANY),
        scratch_shapes=[pltpu.VMEM((2, tm, d), F32), pltpu.VMEM((tm, d), BF16), pltpu.VMEM((tm, d), F32),
                        pltpu.SemaphoreType.DMA((2,)), pltpu.SemaphoreType.DMA((2,))],
    )
    return pl.pallas_call(
        functools.partial(_moe_kernel, n_tokens=n, alpha=alpha),
        grid_spec=grid_spec,
        out_shape=jax.ShapeDtypeStruct((n, d), F32),
        compiler_params=_params("arbitrary", "arbitrary"),
        name="moe_ffn_layer_norm",
    )(blk_cnt, row_tok, step_need, step_expert, h, gate_sorted, wg, wu, wd, ln_g.reshape(1, d), ln_b.reshape(1, d))


def kernel(x, positions, attn_w_in, attn_q_norm, attn_w_uq, attn_w_uq_idx, attn_kidx_norm_g, attn_kidx_norm_b, attn_w_o, hgrn_w_in, hgrn_lower_bounds, hgrn_out_norm, hgrn_w_o, ln_mix_g, ln_mix_b, ln_ffn_g, ln_ffn_b, router_w, router_b, moe_w_gate, moe_w_up, moe_w_down):
    batch, seq, d = x.shape
    n = batch * seq
    depth = ln_mix_g.shape[0]
    alpha = (2 * depth) ** 0.25
    q_rank = attn_w_uq.shape[1]
    n_q = attn_w_uq.shape[2] // HEAD_DIM
    n_kv = n_q // Q_GROUP
    kv_dim = n_kv * HEAD_DIM
    n_idx = attn_w_uq_idx.shape[2] // IDX_DIM
    k_sel = min(INDEX_TOPK, seq // 4)
    att_in = attn_w_in.shape[2]
    att_pad = q_rank + 2 * kv_dim + LANES - att_in
    tk = min(ATTN_BLOCK, seq)

    w_in_a = jnp.pad(attn_w_in, ((0, 0), (0, 0), (0, att_pad))).astype(BF16)
    w_o_a = attn_w_o.astype(BF16)
    w_in_h = hgrn_w_in.astype(BF16)
    w_o_h = hgrn_w_o.astype(BF16)
    d_exp = moe_w_gate.shape[-1]
    wg = moe_w_gate.astype(BF16).reshape(depth * N_EXPERTS, d, d_exp)
    wu = moe_w_up.astype(BF16).reshape(depth * N_EXPERTS, d, d_exp)
    wd = moe_w_down.astype(BF16).reshape(depth * N_EXPERTS, d_exp, d)
    router_w_t = router_w.T.astype(BF16)
    lb_soft = jax.nn.softmax(hgrn_lower_bounds.astype(F32), axis=0)
    lower_bounds = jnp.cumsum(lb_soft, axis=0) - lb_soft[0]

    c128, s128, c64, s64, ct128, st128, ct64, st64 = rope_tables(positions)

    h = x.reshape(n, d)
    for layer in range(depth):
        j = layer // 2
        if layer % 2 == 0:
            proj = matmul(h, w_in_a, j, tn=640)
            q_t = query_projection_t(proj, q_rank, attn_q_norm[j], attn_w_uq[j].T.astype(BF16), ct128, st128,
                                     HEAD_DIM, HEAD_DIM ** -0.5 * LOG2E)
            q_idx_t = query_projection_t(proj, q_rank, attn_q_norm[j], attn_w_uq_idx[j].T.astype(BF16), ct64,
                                         st64, IDX_DIM, 1.0)
            k, v_t, k_a, k_b, w_t = key_value_projection(
                proj, q_rank, kv_dim, attn_kidx_norm_g[j], attn_kidx_norm_b[j], (c128, s128, c64, s64),
                n_idx ** -0.5 * IDX_DIM ** -0.5, tk)
            bias = indexer_bias(q_idx_t, w_t, k_a, k_b, batch, seq, n_idx, k_sel, tk)
            o = masked_attention(q_t, k, v_t, bias, batch, seq, n_kv)
            mix = matmul(o, w_o_a, j)
        else:
            proj = matmul(h, w_in_h, j)
            o = hgrn_recurrence(proj, lower_bounds[layer], hgrn_out_norm[j], batch, seq, d)
            mix = matmul(o, w_o_h, j)
        h, routing = residual_layer_norm_route(h, mix, ln_mix_g[layer], ln_mix_b[layer], alpha,
                                               router_w_t, router_b)
        h = moe_ffn_layer_norm(h, routing, wg, wu, wd, layer, ln_ffn_g[layer], ln_ffn_b[layer], alpha)
    return h.reshape(batch, seq, d)
```
